```python
import math
import jax
import jax.numpy as jnp
from jax import lax
import numpy as np

D_MODEL = 1024
BATCH = 16
SEQ = 2048
DEPTH = 1

GRID_W = 64
CTX_LEN = 256
MIX_WIDTH = D_MODEL
RET_WIDTH = MIX_WIDTH // 2
RET_HEADS = 4
RET_HEAD_DIM = RET_WIDTH // RET_HEADS
RET_CHUNK = 128
RWKV_WIDTH = MIX_WIDTH - RET_WIDTH
RWKV_HEAD_DIM = 64
RWKV_HEADS = RWKV_WIDTH // RWKV_HEAD_DIM
DECAY_LORA = 64
AAA_LORA = 64
GATE_LORA = 128
D_FF = 4 * D_MODEL
ROPE_BASE = 10000.0
NORM_EPS = 1e-6
GN_EPS = 64e-5
W_DECAY_SCALE = math.exp(-0.5)
RET_COLS = 4 * RET_WIDTH
SHIFT_COLS = 3 * RWKV_WIDTH + DECAY_LORA + AAA_LORA + GATE_LORA
IN_COLS = RET_COLS + SHIFT_COLS

kernel_name = "hybrid_retention_rwkv7_dit_layer"


def rmsnorm(x, g):
    xf = x.astype(jnp.float32)
    y = xf * lax.rsqrt(jnp.mean(xf * xf, axis=-1, keepdims=True) + NORM_EPS)
    return (y * g.astype(jnp.float32)).astype(x.dtype)


def adaln_params(cvec, w_ada, b_ada):
    m = jax.nn.silu(cvec) @ w_ada + b_ada
    return jnp.split(m, 6, axis=-1)


def modulate(h, shift, scale):
    return h * (1 + scale) + shift


def flip_t(a):
    return jnp.flip(a, axis=1)


def split_heads(t, n_heads, head_dim):
    return t.reshape(t.shape[0], t.shape[1], n_heads, head_dim).astype(jnp.float32)


def rope_tables(rows, cols):
    half = RET_HEAD_DIM // 2
    inv = jnp.power(ROPE_BASE, -jnp.arange(0, half, 2, dtype=jnp.float32) / half)
    ang_r = rows.astype(jnp.float32)[:, None] * inv[None, :]
    ang_c = cols.astype(jnp.float32)[:, None] * inv[None, :]
    return (jnp.cos(ang_r), jnp.sin(ang_r), jnp.cos(ang_c), jnp.sin(ang_c))


def rotate_block(x, cos, sin):
    x1, x2 = jnp.split(x, 2, axis=-1)
    cos = cos[None, :, None, :]
    sin = sin[None, :, None, :]
    return jnp.concatenate([x1 * cos - x2 * sin, x1 * sin + x2 * cos], axis=-1)


def apply_rope_2d(x, tables):
    cr, sr, cc, sc = tables
    half = x.shape[-1] // 2
    return jnp.concatenate([rotate_block(x[..., :half], cr, sr), rotate_block(x[..., half:], cc, sc)], axis=-1)


def retention_scan(q, k, v, log_gamma, s0, inclusive):
    bsz, t_len, n_h, _ = q.shape
    dv = v.shape[-1]
    c = RET_CHUNK
    n_chunks = t_len // c

    def chunks(a):
        return a.reshape(bsz, n_chunks, c, n_h, a.shape[-1]).transpose(1, 0, 3, 2, 4)

    idx = jnp.arange(c, dtype=jnp.float32)
    dist = idx[:, None] - idx[None, :]
    mask = (dist >= 0) if inclusive else (dist > 0)
    lg = log_gamma[:, None, None]
    intra_decay = jnp.where(mask[None], jnp.exp(lg * jnp.maximum(dist, 0.0)[None]), 0.0)
    q_decay = jnp.exp(log_gamma[:, None] * (idx + 1.0)[None, :])
    k_decay = jnp.exp(log_gamma[:, None] * (c - 1.0 - idx)[None, :])
    chunk_decay = jnp.exp(log_gamma * c)

    def body(state, inp):
        qc, kc, vc = inp
        scores = jnp.einsum('bhid,bhjd->bhij', qc, kc) * intra_decay
        out = (jnp.einsum('bhij,bhjd->bhid', scores, vc)
               + jnp.einsum('bhid,bhde->bhie', qc * q_decay[..., None], state))
        state = (state * chunk_decay[:, None, None]
                 + jnp.einsum('bhjd,bhje->bhde', kc * k_decay[..., None], vc))
        return state, out

    s_final, outs = lax.scan(body, s0, (chunks(q), chunks(k), chunks(v)))
    out = outs.transpose(1, 0, 3, 2, 4).reshape(bsz, t_len, n_h, dv)
    return out, s_final


def retention_bidir(q, k, v, log_g, s0_fwd, s0_bwd):
    o_f, s_f = retention_scan(q, k, v, log_g[0], s0_fwd, True)
    o_b, s_b = retention_scan(flip_t(q), flip_t(k), flip_t(v), log_g[1], s0_bwd, False)
    return o_f + flip_t(o_b), s_f, s_b


def head_rms(o):
    o = o * lax.rsqrt(jnp.mean(o * o, axis=-1, keepdims=True) + NORM_EPS)
    return o.reshape(o.shape[0], o.shape[1], -1)


def token_shift(p, mu):
    prev = jnp.pad(p[:, :-1], ((0, 0), (1, 0), (0, 0)))
    nxt = jnp.pad(p[:, 1:], ((0, 0), (0, 1), (0, 0)))
    return p + mu[0] * (prev - p) + mu[1] * (nxt - p)


def rwkv_prepare(p, shift_mu, w0, w_up, a0, a_up, g_up, k_k, k_a):
    p = token_shift(p, shift_mu)
    w_ = RWKV_WIDTH
    r, k, v, wl, al, gl = jnp.split(
        p, [w_, 2 * w_, 3 * w_, 3 * w_ + DECAY_LORA, 3 * w_ + DECAY_LORA + AAA_LORA], axis=-1)
    kk = split_heads(k * k_k, RWKV_HEADS, RWKV_HEAD_DIM)
    kk = kk * lax.rsqrt(jnp.sum(kk * kk, axis=-1, keepdims=True) + 1e-12)
    dirs = []
    for d in range(2):
        w = jnp.exp(-W_DECAY_SCALE * jax.nn.sigmoid((w0[d] + jnp.tanh(wl) @ w_up[d]).astype(jnp.float32)))
        a = jax.nn.sigmoid((a0[d] + al @ a_up[d]).astype(jnp.float32))
        kt = k.astype(jnp.float32) * (1.0 + (a - 1.0) * k_a.astype(jnp.float32))
        dirs.append((split_heads(w, RWKV_HEADS, RWKV_HEAD_DIM),
                     split_heads(a, RWKV_HEADS, RWKV_HEAD_DIM),
                     split_heads(kt, RWKV_HEADS, RWKV_HEAD_DIM)))
    g = jax.nn.sigmoid(gl) @ g_up
    return (split_heads(r, RWKV_HEADS, RWKV_HEAD_DIM), split_heads(v, RWKV_HEADS, RWKV_HEAD_DIM), kk, dirs, g)


def rwkv7_scan(r, w, kk, a, kt, v, s0, inclusive):
    def update(state, w_t, kk_t, a_t, kt_t, v_t):
        removed = jnp.einsum('bhvk,bhk->bhv', state, kk_t)
        return (state * w_t[:, :, None, :]
                - removed[..., None] * (kk_t * a_t)[:, :, None, :]
                + v_t[..., None] * kt_t[:, :, None, :])

    def body(state, inp):
        r_t, w_t, kk_t, a_t, kt_t, v_t = inp
        if inclusive:
            state = update(state, w_t, kk_t, a_t, kt_t, v_t)
            y = jnp.einsum('bhvk,bhk->bhv', state, r_t)
        else:
            y = jnp.einsum('bhvk,bhk->bhv', state, r_t)
            state = update(state, w_t, kk_t, a_t, kt_t, v_t)
        return state, y

    xs = tuple(jnp.moveaxis(t, 1, 0) for t in (r, w, kk, a, kt, v))
    s_final, ys = lax.scan(body, s0, xs)
    return jnp.moveaxis(ys, 0, 1), s_final


def rwkv_bidir(r, v, kk, dirs, s0_fwd, s0_bwd):
    (w_f, a_f, kt_f), (w_b, a_b, kt_b) = dirs
    y_f, s_f = rwkv7_scan(r, w_f, kk, a_f, kt_f, v, s0_fwd, True)
    y_b, s_b = rwkv7_scan(flip_t(r), flip_t(w_b), flip_t(kk), flip_t(a_b), flip_t(kt_b), flip_t(v), s0_bwd, False)
    return y_f + flip_t(y_b), s_f, s_b


def merge_heads(o_ret, g_ret, y_rw, feat, r_k, ln_w, ln_b, w_out, dtype):
    r, v, _, dirs, g_rw = feat
    ret_out = head_rms(o_ret) * jax.nn.silu(g_ret.astype(jnp.float32))
    mean = jnp.mean(y_rw, axis=-1, keepdims=True)
    var = jnp.var(y_rw, axis=-1, keepdims=True)
    y_n = ((y_rw - mean) * lax.rsqrt(var + GN_EPS)).reshape(y_rw.shape[0], y_rw.shape[1], -1)
    y_n = y_n * ln_w.astype(jnp.float32) + ln_b.astype(jnp.float32)
    kt_f = dirs[0][2]
    rk = r_k.astype(jnp.float32).reshape(RWKV_HEADS, RWKV_HEAD_DIM)
    bonus = (jnp.sum(r * kt_f * rk, axis=-1, keepdims=True) * v).reshape(y_n.shape)
    rw_out = (y_n + bonus) * g_rw.astype(jnp.float32)
    return jnp.concatenate([ret_out, rw_out], axis=-1).astype(dtype) @ w_out


def token_mixers(hx, hc, rope, w_in, log_decay, shift_mu, w0, w_up, a0, a_up, g_up,
                 k_k, k_a, r_k, ln_w, ln_b, w_out, with_ctx_out):
    bsz = hx.shape[0]
    px = hx @ w_in
    pc = hc @ w_in

    log_g = -jnp.exp(log_decay.astype(jnp.float32))
    k_scale = RET_HEAD_DIM ** -0.5

    def ret_qkvg(p):
        q, k, v, g = jnp.split(p[..., :RET_COLS], 4, axis=-1)
        return (split_heads(q, RET_HEADS, RET_HEAD_DIM), split_heads(k, RET_HEADS, RET_HEAD_DIM) * k_scale,
                split_heads(v, RET_HEADS, RET_HEAD_DIM), g)

    qc, kc, vc, gc = ret_qkvg(pc)
    qx, kx, vx, gx = ret_qkvg(px)
    qx = apply_rope_2d(qx, rope)
    kx = apply_rope_2d(kx, rope)
    zeros_ret = jnp.zeros((bsz, RET_HEADS, RET_HEAD_DIM, RET_HEAD_DIM), jnp.float32)
    oc_ret, sf_ret, sb_ret = retention_bidir(qc, kc, vc, log_g, zeros_ret, zeros_ret)
    ox_ret, _, _ = retention_bidir(qx, kx, vx, log_g, sf_ret, sb_ret)

    feat_c = rwkv_prepare(pc[..., RET_COLS:], shift_mu, w0, w_up, a0, a_up, g_up, k_k, k_a)
    feat_x = rwkv_prepare(px[..., RET_COLS:], shift_mu, w0, w_up, a0, a_up, g_up, k_k, k_a)
    zeros_rw = jnp.zeros((bsz, RWKV_HEADS, RWKV_HEAD_DIM, RWKV_HEAD_DIM), jnp.float32)
    yc_rw, sf_rw, sb_rw = rwkv_bidir(feat_c[0], feat_c[1], feat_c[2], feat_c[3], zeros_rw, zeros_rw)
    yx_rw, _, _ = rwkv_bidir(feat_x[0], feat_x[1], feat_x[2], feat_x[3], sf_rw, sb_rw)

    out_x = merge_heads(ox_ret, gx, yx_rw, feat_x, r_k, ln_w, ln_b, w_out, hx.dtype)
    out_c = merge_heads(oc_ret, gc, yc_rw, feat_c, r_k, ln_w, ln_b, w_out, hc.dtype) if with_ctx_out else None
    return out_x, out_c


def squared_relu_mlp(h, w1, b1, w2, b2):
    return jnp.square(jax.nn.relu(h @ w1 + b1)) @ w2 + b2


def setup_inputs(seed: int = 0) -> dict:
    key = jax.random.key(seed)
    ks = jax.random.split(key, 32)
    L, D, W = DEPTH, D_MODEL, RWKV_WIDTH

    def nrm(k, shape, s):
        return jax.random.normal(k, shape, jnp.float32) * s

    base_decay = jnp.log(-jnp.log(1.0 - jnp.power(2.0, -5.0 - jnp.arange(RET_HEADS, dtype=jnp.float32))))
    return {
        'x': nrm(ks[0], (BATCH, SEQ, D), 1.0),
        'c': nrm(ks[1], (BATCH, D), 1.0),
        'ctx': nrm(ks[2], (BATCH, CTX_LEN, D), 1.0),
        'c_ctx': nrm(ks[3], (D,), 1.0),
        'w_ada': nrm(ks[4], (L, D, 6 * D), D ** -0.5),
        'b_ada': nrm(ks[5], (L, 6 * D), 0.02),
        'norm1_g': 1.0 + nrm(ks[6], (L, D), 0.02),
        'norm2_g': 1.0 + nrm(ks[7], (L, D), 0.02),
        'w_in': nrm(ks[8], (L, D, IN_COLS), D ** -0.5),
        'ret_log_decay': base_decay + nrm(ks[9], (L, 2, RET_HEADS), 0.05),
        'rwkv_shift_mu': jax.random.uniform(ks[10], (L, 2, SHIFT_COLS), jnp.float32, 0.0, 0.5),
        'rwkv_w0': jax.random.uniform(ks[11], (L, 2, W), jnp.float32, -3.0, 1.0),
        'rwkv_w_up': nrm(ks[12], (L, 2, DECAY_LORA, W), 0.5 * DECAY_LORA ** -0.5),
        'rwkv_a0': nrm(ks[13], (L, 2, W), 0.5),
        'rwkv_a_up': nrm(ks[14], (L, 2, AAA_LORA, W), 0.5 * AAA_LORA ** -0.5),
        'rwkv_g_up': nrm(ks[15], (L, GATE_LORA, W), GATE_LORA ** -0.5),
        'rwkv_k_k': 0.85 + nrm(ks[16], (L, W), 0.05),
        'rwkv_k_a': 1.0 + nrm(ks[17], (L, W), 0.05),
        'rwkv_r_k': nrm(ks[18], (L, W), 0.1),
        'rwkv_ln_w': 1.0 + nrm(ks[19], (L, W), 0.02),
        'rwkv_ln_b': nrm(ks[20], (L, W), 0.02),
        'w_out': nrm(ks[21], (L, D, D), D ** -0.5),
        'w_ff1': nrm(ks[22], (L, D, D_FF), D ** -0.5),
        'b_ff1': nrm(ks[23], (L, D_FF), 0.02),
        'w_ff2': nrm(ks[24], (L, D_FF, D), D_FF ** -0.5),
        'b_ff2': nrm(ks[25], (L, D), 0.02),
        'final_g': 1.0 + nrm(ks[26], (D,), 0.02),
    }


def reference(x, c, ctx, c_ctx, w_ada, b_ada, norm1_g, norm2_g, w_in, ret_log_decay,
              rwkv_shift_mu, rwkv_w0, rwkv_w_up, rwkv_a0, rwkv_a_up, rwkv_g_up,
              rwkv_k_k, rwkv_k_a, rwkv_r_k, rwkv_ln_w, rwkv_ln_b, w_out,
              w_ff1, b_ff1, w_ff2, b_ff2, final_g):
    n_tokens = x.shape[1]
    ROWS = n_tokens // GRID_W
    rows = jnp.repeat(jnp.arange(ROWS), GRID_W)
    cols = jnp.tile(jnp.arange(GRID_W), ROWS)
    rope = rope_tables(rows, cols)

    h_x, h_c = x, ctx
    for l in range(DEPTH):
        with_ctx = l < DEPTH - 1
        sh1, sc1, g1, sh2, sc2, g2 = [m[:, None, :] for m in adaln_params(c, w_ada[l], b_ada[l])]
        csh1, csc1, cg1, csh2, csc2, cg2 = adaln_params(c_ctx, w_ada[l], b_ada[l])
        nx = modulate(rmsnorm(h_x, norm1_g[l]), sh1, sc1)
        nc = modulate(rmsnorm(h_c, norm1_g[l]), csh1, csc1)
        mix_x, mix_c = token_mixers(nx, nc, rope, w_in[l], ret_log_decay[l], rwkv_shift_mu[l],
                                    rwkv_w0[l], rwkv_w_up[l], rwkv_a0[l], rwkv_a_up[l], rwkv_g_up[l],
                                    rwkv_k_k[l], rwkv_k_a[l], rwkv_r_k[l], rwkv_ln_w[l], rwkv_ln_b[l],
                                    w_out[l], with_ctx)
        h_x = h_x + g1 * mix_x
        h_x = h_x + g2 * squared_relu_mlp(modulate(rmsnorm(h_x, norm2_g[l]), sh2, sc2),
                                          w_ff1[l], b_ff1[l], w_ff2[l], b_ff2[l])
        if with_ctx:
            h_c = h_c + cg1 * mix_c
            h_c = h_c + cg2 * squared_relu_mlp(modulate(rmsnorm(h_c, norm2_g[l]), csh2, csc2),
                                              w_ff1[l], b_ff1[l], w_ff2[l], b_ff2[l])
    return rmsnorm(h_x, final_g)
```

```python
import functools
import math

import jax
import jax.numpy as jnp
from jax import lax
from jax.experimental import pallas as pl
from jax.experimental.pallas import tpu as pltpu

F32 = jnp.float32
BF16 = jnp.bfloat16

GRID_W = 64
RET_HEADS = 4
RET_HEAD_DIM = 128
RET_WIDTH = RET_HEADS * RET_HEAD_DIM
RET_CHUNK = 128
RWKV_HEAD_DIM = 64
RWKV_WIDTH = 512
RWKV_PAIRS = RWKV_WIDTH // (2 * RWKV_HEAD_DIM)
RWKV_CHUNK = 64
DECAY_LORA = 64
AAA_LORA = 64
GATE_LORA = 128
SHIFT_COLS = 3 * RWKV_WIDTH + DECAY_LORA + AAA_LORA + GATE_LORA
RET_COLS = 4 * RET_WIDTH
ROPE_BASE = 10000.0
NORM_EPS = 1e-6
GN_EPS = 64e-5
W_DECAY_SCALE = math.exp(-0.5)
LANES = 128
SUBLANES = 8
VMEM_LIMIT = 48 * 1024 * 1024


def _params(*sem):
    return pltpu.CompilerParams(dimension_semantics=sem, vmem_limit_bytes=VMEM_LIMIT)


def _mm(a, b):
    return jnp.dot(a.astype(BF16), b.astype(BF16), preferred_element_type=F32)


def _mm_nt(a, b):
    return lax.dot_general(a.astype(BF16), b.astype(BF16), (((1,), (1,)), ((), ())),
                           preferred_element_type=F32)


def _mm_tn(a, b):
    return lax.dot_general(a.astype(BF16), b.astype(BF16), (((0,), (0,)), ((), ())),
                           preferred_element_type=F32)


def _split3(x):
    hi = x.astype(BF16)
    r1 = x - hi.astype(F32)
    mid = r1.astype(BF16)
    lo = (r1 - mid.astype(F32)).astype(BF16)
    return hi, mid, lo


def _mm_exact_rhs(a_bf16, x):
    hi, mid, lo = _split3(x)
    d = lambda p: jnp.dot(a_bf16, p, preferred_element_type=F32)
    return d(hi) + d(mid) + d(lo)


def _mm_exact_lhs(x, a_bf16):
    hi, mid, lo = _split3(x)
    d = lambda p: jnp.dot(p, a_bf16, preferred_element_type=F32)
    return d(hi) + d(mid) + d(lo)


def _rms(x, g):
    return x * lax.rsqrt(jnp.mean(x * x, axis=-1, keepdims=True) + NORM_EPS) * g


def _ada_kernel(c_ref, w_ref, b_ref, o_ref):
    s = jax.nn.silu(c_ref[...])
    o_ref[...] = jnp.dot(s, w_ref[...], preferred_element_type=F32,
                         precision=lax.Precision.HIGHEST) + b_ref[...]


def _ada(cc, w, b):
    rows, d = cc.shape
    n = w.shape[1]
    tn = n // 4
    return pl.pallas_call(
        _ada_kernel,
        grid=(n // tn,),
        in_specs=[pl.BlockSpec((rows, d), lambda j: (0, 0)),
                  pl.BlockSpec((d, tn), lambda j: (0, j)),
                  pl.BlockSpec((1, tn), lambda j: (0, j))],
        out_specs=pl.BlockSpec((rows, tn), lambda j: (0, j)),
        out_shape=jax.ShapeDtypeStruct((rows, n), F32),
        compiler_params=_params("arbitrary"),
        name="adaln",
    )(cc, w, b)


def _inproj_kernel(x_ref, sh_ref, sc_ref, g_ref, wr_ref, ww_ref, pret_ref, prw_ref):
    h = _rms(x_ref[0], g_ref[...]) * (1.0 + sc_ref[0]) + sh_ref[0]
    hb = h.astype(BF16)
    pret_ref[0] = jnp.dot(hb, wr_ref[...], preferred_element_type=F32)
    prw_ref[0] = jnp.dot(hb, ww_ref[...], preferred_element_type=F32)


def _inproj(x, shift, scale, g, w_ret, w_rw, tm):
    bsz, t, d = x.shape
    const = lambda b, i: (0, 0)
    return pl.pallas_call(
        _inproj_kernel,
        grid=(bsz, t // tm),
        in_specs=[pl.BlockSpec((1, tm, d), lambda b, i: (b, i, 0)),
                  pl.BlockSpec((1, 1, d), lambda b, i: (b, 0, 0)),
                  pl.BlockSpec((1, 1, d), lambda b, i: (b, 0, 0)),
                  pl.BlockSpec((1, d), const),
                  pl.BlockSpec(w_ret.shape, const),
                  pl.BlockSpec(w_rw.shape, const)],
        out_specs=[pl.BlockSpec((1, tm, RET_COLS), lambda b, i: (b, i, 0)),
                   pl.BlockSpec((1, tm, SHIFT_COLS), lambda b, i: (b, i, 0))],
        out_shape=[jax.ShapeDtypeStruct((bsz, t, RET_COLS), F32),
                   jax.ShapeDtypeStruct((bsz, t, SHIFT_COLS), F32)],
        compiler_params=_params("parallel", "parallel"),
        name="inproj",
    )(x, shift, scale, g, w_ret, w_rw)


def _ret_kernel(q_ref, k_ref, v_ref, g_ref, kc_ref, vc_ref, cos_ref, sin_ref, ld_ref, o_ref,
                qr_s, kr_s, sb_s):
    c = RET_CHUNK
    t = q_ref.shape[1]
    tc = kc_ref.shape[1]
    n = t // c
    k_scale = RET_HEAD_DIM ** -0.5
    lgf = -jnp.exp(ld_ref[0, 0][0:1, :])
    lgb = -jnp.exp(ld_ref[1, 0][0:1, :])

    jc = lax.broadcasted_iota(jnp.int32, (tc, 1), 0).astype(F32)
    kc = kc_ref[0] * k_scale
    vc = vc_ref[0]
    s_f0 = _mm_tn(kc * jnp.exp(lgf * (tc - 1.0 - jc)), vc)
    s_b0 = _mm_tn(kc * jnp.exp(lgb * jc), vc)

    lane = lax.broadcasted_iota(jnp.int32, (c, RET_HEAD_DIM), 1)
    first_half = (lane % 64) < 32

    def rope(x, cs, sn):
        swapped = jnp.where(first_half, pltpu.roll(x, 96, axis=1), pltpu.roll(x, 32, axis=1))
        return x * cs + swapped * sn

    def rope_body(i, carry):
        sl = pl.ds(pl.multiple_of(i * c, c), c)
        cs = cos_ref[sl, :]
        sn = sin_ref[sl, :]
        qr_s[sl, :] = rope(q_ref[0, sl, :], cs, sn)
        kr_s[sl, :] = rope(k_ref[0, sl, :] * k_scale, cs, sn)
        return carry

    lax.fori_loop(0, n, rope_body, 0)

    ii = lax.broadcasted_iota(jnp.int32, (c, 1), 0).astype(F32)
    dist = (lax.broadcasted_iota(jnp.int32, (c, c), 0)
            - lax.broadcasted_iota(jnp.int32, (c, c), 1)).astype(F32)
    decay = jnp.where(dist >= 0, jnp.exp(lgf * jnp.maximum(dist, 0.0)),
                      jnp.exp(lgb * jnp.maximum(-dist, 0.0)))
    qdec_f = jnp.exp(lgf * (ii + 1.0))
    kdec_f = jnp.exp(lgf * (c - 1.0 - ii))
    qdec_b = jnp.exp(lgb * (c - ii))
    kdec_b = jnp.exp(lgb * ii)
    cdec_f = jnp.exp(lgf * c)
    cdec_b = jnp.exp(lgb * c)

    def bwd_body(i, s_b):
        ci = n - 1 - i
        sl = pl.ds(pl.multiple_of(ci * c, c), c)
        sb_s[ci] = s_b
        return s_b * cdec_b + _mm_tn(kr_s[sl, :] * kdec_b, v_ref[0, sl, :])

    lax.fori_loop(0, n, bwd_body, s_b0)

    def fwd_body(ci, s_f):
        sl = pl.ds(pl.multiple_of(ci * c, c), c)
        qc = qr_s[sl, :]
        kx = kr_s[sl, :]
        vx = v_ref[0, sl, :]
        scores = _mm_nt(qc, kx) * decay
        o = _mm(scores, vx) + _mm(qc * qdec_f, s_f) + _mm(qc * qdec_b, sb_s[ci])
        o = o * lax.rsqrt(jnp.mean(o * o, axis=-1, keepdims=True) + NORM_EPS)
        o_ref[0, sl, :] = (o * jax.nn.silu(g_ref[0, sl, :])).astype(o_ref.dtype)
        return s_f * cdec_f + _mm_tn(kx * kdec_f, vx)

    lax.fori_loop(0, n, fwd_body, s_f0)


def _retention(p_ret, pc_ret, cos_t, sin_t, ld):
    bsz, t, _ = p_ret.shape
    tc = pc_ret.shape[1]
    hd = RET_HEAD_DIM
    col = lambda j: pl.BlockSpec((1, t, hd), lambda b, h: (b, 0, h + RET_HEADS * j))
    colc = lambda j: pl.BlockSpec((1, tc, hd), lambda b, h: (b, 0, h + RET_HEADS * j))
    return pl.pallas_call(
        _ret_kernel,
        grid=(bsz, RET_HEADS),
        in_specs=[col(0), col(1), col(2), col(3), colc(1), colc(2),
                  pl.BlockSpec((t, hd), lambda b, h: (0, 0)),
                  pl.BlockSpec((t, hd), lambda b, h: (0, 0)),
                  pl.BlockSpec((2, 1, SUBLANES, LANES), lambda b, h: (0, h, 0, 0))],
        out_specs=pl.BlockSpec((1, t, hd), lambda b, h: (b, 0, h)),
        out_shape=jax.ShapeDtypeStruct((bsz, t, RET_WIDTH), BF16),
        scratch_shapes=[pltpu.VMEM((t, hd), F32), pltpu.VMEM((t, hd), F32),
                        pltpu.VMEM((t // RET_CHUNK, hd, hd), F32)],
        compiler_params=_params("parallel", "parallel"),
        name="retention",
    )(p_ret, p_ret, p_ret, p_ret, pc_ret, pc_ret, cos_t, sin_t, ld)


def _pair_expand(y, m0):
    zero = jnp.zeros_like(y)
    return jnp.concatenate([jnp.where(m0, y, zero), jnp.where(m0, zero, y)], axis=0)


def _mm_pair(a, y, m0):
    return _mm(a, _pair_expand(y, m0))


def _split2(x):
    hi = x.astype(BF16)
    return hi, (x - hi.astype(F32)).astype(BF16)


def _mm_pair_x3(a, y, m0):
    ah, al = _split2(a)
    yh, yl = _split2(y)
    eh = _pair_expand(yh, m0)
    return jnp.dot(jnp.concatenate([ah, al, ah], axis=1),
                   jnp.concatenate([eh, eh, _pair_expand(yl, m0)], axis=0),
                   preferred_element_type=F32)


def _rwkv_terms_kernel(p_ref, pp_ref, pn_ref, mu_ref, w0_ref, wup_ref, a0_ref, aup_ref, gup_ref,
                       kk_ref, ka_ref, rk_ref, seg_ref,
                       phi_ref, z_ref, gf_ref, gb_ref, y0_ref, bonus_ref, gate_ref, sh_s):
    c = RWKV_CHUNK
    w_ = RWKV_WIDTH
    hd = RWKV_HEAD_DIM
    ci = pl.program_id(1)
    nchunk = pl.num_programs(1)

    sh_s[0:SUBLANES, :] = jnp.where(ci == 0, 0.0, pp_ref[0])
    sh_s[SUBLANES:SUBLANES + c, :] = p_ref[0]
    sh_s[SUBLANES + c:2 * SUBLANES + c, :] = jnp.where(ci == nchunk - 1, 0.0, pn_ref[0])
    p = p_ref[0]
    prev = sh_s[SUBLANES - 1:SUBLANES - 1 + c, :]
    nxt = sh_s[SUBLANES + 1:SUBLANES + 1 + c, :]
    ps = p + mu_ref[0:1, :] * (prev - p) + mu_ref[1:2, :] * (nxt - p)

    r = ps[:, 0:w_]
    k = ps[:, w_:2 * w_]
    v = ps[:, 2 * w_:3 * w_]
    wl = ps[:, 3 * w_:3 * w_ + DECAY_LORA]
    al = ps[:, 3 * w_ + DECAY_LORA:3 * w_ + DECAY_LORA + AAA_LORA]
    gl = ps[:, 3 * w_ + DECAY_LORA + AAA_LORA:]

    seg = seg_ref[...]
    kk = k * kk_ref[...]
    kk = kk * lax.rsqrt(_mm_exact_lhs(kk * kk, seg) + 1e-12)
    w_pre = _mm(jnp.tanh(wl), wup_ref[...])
    a_pre = _mm(al, aup_ref[...])
    gate_ref[0] = _mm(jax.nn.sigmoid(gl), gup_ref[...])

    row = lax.broadcasted_iota(jnp.int32, (c, c), 0)
    colm = lax.broadcasted_iota(jnp.int32, (c, c), 1)
    lane = lax.broadcasted_iota(jnp.int32, (c, LANES), 1)
    trow = lax.broadcasted_iota(jnp.int32, (c, LANES), 0)
    m0 = lane < hd
    jj = lane % hd
    r2 = lax.broadcasted_iota(jnp.int32, (LANES, LANES), 0)
    c2 = lax.broadcasted_iota(jnp.int32, (LANES, LANES), 1)
    blockdiag = (r2 // hd) == (c2 // hd)
    eye = r2 == c2

    eye_cat = jnp.where(jj == trow, 1.0, 0.0)
    chains = []
    for d in range(2):
        fwd = d == 0
        logw = -W_DECAY_SCALE * jax.nn.sigmoid(w0_ref[d:d + 1, :] + w_pre[:, d * w_:(d + 1) * w_])
        a = jax.nn.sigmoid(a0_ref[d:d + 1, :] + a_pre[:, d * w_:(d + 1) * w_])
        kt = k * (1.0 + (a - 1.0) * ka_ref[...])
        b = kk * a
        if fwd:
            bonus_ref[0] = _mm_exact_lhs(r * kt * rk_ref[...], seg) * v
            tri = (colm <= row).astype(BF16)
            strict = jj < trow
            omask = jj <= trow
        else:
            tri = (colm >= row).astype(BF16)
            strict = jj > trow
            omask = strict
        big_l = _mm_exact_rhs(tri, logw)
        lx = big_l - logw
        ltot = big_l[c - 1:c, :] if fwd else big_l[0:1, :]
        e_lx = jnp.exp(lx)
        e_nl = jnp.exp(-big_l)
        at = -kk * e_lx
        bt = b * e_nl
        ktl = kt * e_nl
        rt = r * (jnp.exp(big_l) if fwd else e_lx)
        e_rest = jnp.exp(ltot - big_l)
        bh = b * e_rest
        kh = kt * e_rest
        e_tot = jnp.exp(ltot)
        for pr in range(RWKV_PAIRS):
            sl = slice(pr * LANES, (pr + 1) * LANES)
            at_p, rt_p, v_p = at[:, sl], rt[:, sl], v[:, sl]
            m = _mm_nt(jnp.concatenate([at_p, rt_p], axis=0),
                       jnp.concatenate([_pair_expand(bt[:, sl], m0), _pair_expand(ktl[:, sl], m0)], axis=0))
            chains.append(dict(
                d=d, sl=sl, pr=pr, at=at_p, rt=rt_p, v=v_p, bh=bh[:, sl], kh=kh[:, sl], e_tot=e_tot[:, sl],
                a_ab=jnp.where(strict, m[:c, :LANES], 0.0), a_ak=jnp.where(strict, m[:c, LANES:], 0.0),
                m_rb=jnp.where(omask, m[c:, :LANES], 0.0), m_rk=jnp.where(omask, m[c:, LANES:], 0.0)))

    tinv = [eye_cat + ch["a_ab"] for ch in chains]
    pw = [ch["a_ab"] for ch in chains]
    span = 2
    while span < c:
        mm = _mm_pair if 2 * span >= c else _mm_pair_x3
        pw = [mm(p, p, m0) for p in pw]
        tinv = [t + mm(t, p, m0) for t, p in zip(tinv, pw)]
        span *= 2

    y0 = [jnp.zeros((c, LANES), F32) for _ in range(RWKV_PAIRS)]
    for ch, t_m in zip(chains, tinv):
        bd_v = _pair_expand(ch["v"], m0)
        akv = _mm(ch["a_ak"], bd_v)
        pq = _mm(t_m, jnp.concatenate([_pair_expand(ch["at"], m0), _pair_expand(akv, m0)], axis=1))
        p_m, q_m = pq[:, :LANES], pq[:, LANES:]
        gy = _mm(ch["m_rb"], jnp.concatenate([_pair_expand(p_m, m0), _pair_expand(q_m, m0)], axis=1))
        g_ref = gf_ref if ch["d"] == 0 else gb_ref
        g_ref[0, :, ch["sl"]] = ch["rt"] + gy[:, :LANES]
        y0[ch["pr"]] = y0[ch["pr"]] + gy[:, LANES:] + _mm(ch["m_rk"], bd_v)
        pz = _mm_tn(pq, ch["bh"])
        phi = pz[:LANES] + jnp.where(eye, ch["e_tot"], 0.0)
        zz = pz[LANES:] + _mm_tn(ch["v"], ch["kh"])
        phi_ref[0, 0, ch["d"], ch["pr"]] = jnp.where(blockdiag, phi, 0.0)
        z_ref[0, 0, ch["d"], ch["pr"]] = jnp.where(blockdiag, zz, 0.0)
    for pr in range(RWKV_PAIRS):
        y0_ref[0, :, pr * LANES:(pr + 1) * LANES] = y0[pr]


def _rwkv_terms(p_rw, mu, w0, wup, a0, aup, gup, k_k, k_a, r_k, seg):
    bsz, t, cols = p_rw.shape
    c = RWKV_CHUNK
    n = t // c
    blk8 = c // SUBLANES
    last8 = t // SUBLANES - 1
    const = lambda b, i: (0, 0)
    tok = pl.BlockSpec((1, c, RWKV_WIDTH), lambda b, i: (b, i, 0))
    mat = pl.BlockSpec((1, 1, 2, RWKV_PAIRS, LANES, LANES), lambda b, i: (b, i, 0, 0, 0, 0))
    tok_shape = jax.ShapeDtypeStruct((bsz, t, RWKV_WIDTH), F32)
    mat_shape = jax.ShapeDtypeStruct((bsz, n, 2, RWKV_PAIRS, LANES, LANES), F32)
    return pl.pallas_call(
        _rwkv_terms_kernel,
        grid=(bsz, n),
        in_specs=[pl.BlockSpec((1, c, cols), lambda b, i: (b, i, 0)),
                  pl.BlockSpec((1, SUBLANES, cols), lambda b, i: (b, jnp.maximum(i * blk8 - 1, 0), 0)),
                  pl.BlockSpec((1, SUBLANES, cols), lambda b, i: (b, jnp.minimum((i + 1) * blk8, last8), 0)),
                  pl.BlockSpec(mu.shape, const), pl.BlockSpec(w0.shape, const),
                  pl.BlockSpec(wup.shape, const), pl.BlockSpec(a0.shape, const),
                  pl.BlockSpec(aup.shape, const), pl.BlockSpec(gup.shape, const),
                  pl.BlockSpec(k_k.shape, const), pl.BlockSpec(k_a.shape, const),
                  pl.BlockSpec(r_k.shape, const), pl.BlockSpec(seg.shape, const)],
        out_specs=[mat, mat, tok, tok, tok, tok, tok],
        out_shape=[mat_shape, mat_shape, tok_shape, tok_shape, tok_shape, tok_shape, tok_shape],
        scratch_shapes=[pltpu.VMEM((c + 2 * SUBLANES, cols), F32)],
        compiler_params=_params("parallel", "parallel"),
        name="rwkv_terms",
    )(p_rw, p_rw, p_rw, mu, w0, wup, a0, aup, gup, k_k, k_a, r_k, seg)


def _rwkv_state_kernel(phif_ref, zf_ref, phib_ref, zb_ref, s0_ref, sf_ref, sb_ref, fin_ref, st_s):
    j = pl.program_id(1)
    n = pl.num_programs(1)

    @pl.when(j == 0)
    def _():
        st_s[...] = s0_ref[0]

    for pr in range(RWKV_PAIRS):
        s_f = st_s[0, pr]
        s_b = st_s[1, pr]
        sf_ref[0, 0, pr] = s_f
        sb_ref[0, 0, pr] = s_b
        st_s[0, pr] = _mm(s_f, phif_ref[0, 0, 0, pr]) + zf_ref[0, 0, 0, pr]
        st_s[1, pr] = _mm(s_b, phib_ref[0, 0, 0, pr]) + zb_ref[0, 0, 0, pr]

    @pl.when(j == n - 1)
    def _():
        fin_ref[0] = st_s[...]


def _rwkv_states(phi, z, s0):
    bsz, n = phi.shape[0], phi.shape[1]
    blk = (1, 1, 1, RWKV_PAIRS, LANES, LANES)
    fwd = pl.BlockSpec(blk, lambda b, j: (b, j, 0, 0, 0, 0))
    bwd = pl.BlockSpec(blk, lambda b, j: (b, n - 1 - j, 1, 0, 0, 0))
    st = (1, 1, RWKV_PAIRS, LANES, LANES)
    st_shape = jax.ShapeDtypeStruct((bsz, n, RWKV_PAIRS, LANES, LANES), F32)
    full = pl.BlockSpec((1, 2, RWKV_PAIRS, LANES, LANES), lambda b, j: (b, 0, 0, 0, 0))
    return pl.pallas_call(
        _rwkv_state_kernel,
        grid=(bsz, n),
        in_specs=[fwd, fwd, bwd, bwd, full],
        out_specs=[pl.BlockSpec(st, lambda b, j: (b, j, 0, 0, 0)),
                   pl.BlockSpec(st, lambda b, j: (b, n - 1 - j, 0, 0, 0)),
                   full],
        out_shape=[st_shape, st_shape,
                   jax.ShapeDtypeStruct((bsz, 2, RWKV_PAIRS, LANES, LANES), F32)],
        scratch_shapes=[pltpu.VMEM((2, RWKV_PAIRS, LANES, LANES), F32)],
        compiler_params=_params("parallel", "arbitrary"),
        name="rwkv_states",
    )(phi, z, phi, z, s0)


def _rwkv_out_kernel(gf_ref, gb_ref, y0_ref, bonus_ref, gate_ref, sf_ref, sb_ref,
                     lnw_ref, lnb_ref, seg_ref, o_ref):
    ys = []
    for pr in range(RWKV_PAIRS):
        sl = slice(pr * LANES, (pr + 1) * LANES)
        ys.append(y0_ref[0, :, sl] + _mm_nt(gf_ref[0, :, sl], sf_ref[0, 0, pr])
                  + _mm_nt(gb_ref[0, :, sl], sb_ref[0, 0, pr]))
    y = jnp.concatenate(ys, axis=1)
    seg = seg_ref[...]
    inv_n = 1.0 / RWKV_HEAD_DIM
    mean = _mm_exact_lhs(y, seg) * inv_n
    yc = y - mean
    var = _mm_exact_lhs(yc * yc, seg) * inv_n
    y_n = yc * lax.rsqrt(var + GN_EPS) * lnw_ref[...] + lnb_ref[...]
    o_ref[0] = ((y_n + bonus_ref[0]) * gate_ref[0]).astype(o_ref.dtype)


def _rwkv_out(gf, gb, y0, bonus, gate, sf, sb, ln_w, ln_b, seg):
    bsz, t, w_ = gf.shape
    c = RWKV_CHUNK
    const = lambda b, i: (0, 0)
    tok = pl.BlockSpec((1, c, w_), lambda b, i: (b, i, 0))
    st = pl.BlockSpec((1, 1, RWKV_PAIRS, LANES, LANES), lambda b, i: (b, i, 0, 0, 0))
    return pl.pallas_call(
        _rwkv_out_kernel,
        grid=(bsz, t // c),
        in_specs=[tok, tok, tok, tok, tok, st, st,
                  pl.BlockSpec(ln_w.shape, const), pl.BlockSpec(ln_b.shape, const),
                  pl.BlockSpec(seg.shape, const)],
        out_specs=tok,
        out_shape=jax.ShapeDtypeStruct((bsz, t, w_), BF16),
        compiler_params=_params("parallel", "parallel"),
        name="rwkv_out",
    )(gf, gb, y0, bonus, gate, sf, sb, ln_w, ln_b, seg)


def _mlp_kernel(x_ref, ret_ref, rw_ref, g1_ref, sh2_ref, sc2_ref, g2_ref, n2g_ref, fg_ref,
                woa_ref, wob_ref, w1_ref, b1_ref, w2_ref, b2_ref, o_ref, *, ff_tile):
    mix = (jnp.dot(ret_ref[0], woa_ref[...], preferred_element_type=F32)
           + jnp.dot(rw_ref[0], wob_ref[...], preferred_element_type=F32))
    h1 = x_ref[0] + g1_ref[0] * mix
    n2 = (_rms(h1, n2g_ref[...]) * (1.0 + sc2_ref[0]) + sh2_ref[0]).astype(BF16)
    d_ff = w1_ref.shape[1]
    acc = jnp.zeros(h1.shape, F32)
    for s in range(d_ff // ff_tile):
        sl = slice(s * ff_tile, (s + 1) * ff_tile)
        hid = jnp.dot(n2, w1_ref[:, sl], preferred_element_type=F32) + b1_ref[:, sl]
        hid = jnp.square(jnp.maximum(hid, 0.0)).astype(BF16)
        acc = acc + jnp.dot(hid, w2_ref[sl, :], preferred_element_type=F32)
    h2 = h1 + g2_ref[0] * (acc + b2_ref[...])
    o_ref[0] = _rms(h2, fg_ref[...])


def _mlp(x, ret, rw, g1, sh2, sc2, g2, n2g, fg, woa, wob, w1, b1, w2, b2, tm):
    bsz, t, d = x.shape
    const = lambda b, i: (0, 0)
    single = pl.Buffered(1)
    mod = pl.BlockSpec((1, 1, d), lambda b, i: (b, 0, 0))
    cw = lambda a: pl.BlockSpec(a.shape, const, pipeline_mode=single)
    return pl.pallas_call(
        functools.partial(_mlp_kernel, ff_tile=1024),
        grid=(bsz, t // tm),
        in_specs=[pl.BlockSpec((1, tm, d), lambda b, i: (b, i, 0)),
                  pl.BlockSpec((1, tm, RET_WIDTH), lambda b, i: (b, i, 0)),
                  pl.BlockSpec((1, tm, RWKV_WIDTH), lambda b, i: (b, i, 0)),
                  mod, mod, mod, mod, cw(n2g), cw(fg),
                  cw(woa), cw(wob), cw(w1), cw(b1), cw(w2), cw(b2)],
        out_specs=pl.BlockSpec((1, tm, d), lambda b, i: (b, i, 0)),
        out_shape=jax.ShapeDtypeStruct((bsz, t, d), F32),
        compiler_params=_params("parallel", "parallel"),
        name="outproj_mlp",
    )(x, ret, rw, g1, sh2, sc2, g2, n2g, fg, woa, wob, w1, b1, w2, b2)


def _rope_tables(t):
    half = RET_HEAD_DIM // 2
    inv = jnp.power(ROPE_BASE, -jnp.arange(0, half, 2, dtype=F32) / half)
    pos = jnp.arange(t)
    ang_r = (pos // GRID_W).astype(F32)[:, None] * inv[None, :]
    ang_c = (pos % GRID_W).astype(F32)[:, None] * inv[None, :]
    cos_t = jnp.concatenate([jnp.cos(ang_r)] * 2 + [jnp.cos(ang_c)] * 2, axis=-1)
    sin_t = jnp.concatenate([-jnp.sin(ang_r), jnp.sin(ang_r), -jnp.sin(ang_c), jnp.sin(ang_c)], axis=-1)
    return cos_t, sin_t


def kernel(x, c, ctx, c_ctx, w_ada, b_ada, norm1_g, norm2_g, w_in, ret_log_decay, rwkv_shift_mu,
           rwkv_w0, rwkv_w_up, rwkv_a0, rwkv_a_up, rwkv_g_up, rwkv_k_k, rwkv_k_a, rwkv_r_k,
           rwkv_ln_w, rwkv_ln_b, w_out, w_ff1, b_ff1, w_ff2, b_ff2, final_g):
    bsz, t, d = x.shape
    assert w_ada.shape[0] == 1, "single trunk layer"
    l = 0

    rows = -(-(bsz + 1) // SUBLANES) * SUBLANES
    cc = jnp.zeros((rows, d), F32).at[:bsz].set(c).at[bsz].set(c_ctx)
    mods = _ada(cc, w_ada[l], b_ada[l][None, :])
    sh1, sc1, g1, sh2, sc2, g2 = [m[:bsz, None, :] for m in jnp.split(mods, 6, axis=-1)]
    csh1, csc1 = [jnp.broadcast_to(m[bsz][None, None, :], (bsz, 1, d))
                  for m in jnp.split(mods, 6, axis=-1)[:2]]

    w_in_b = w_in[l].astype(BF16)
    w_ret, w_rw = w_in_b[:, :RET_COLS], w_in_b[:, RET_COLS:]
    n1g = norm1_g[l][None, :]
    p_ret, p_rw = _inproj(x, sh1, sc1, n1g, w_ret, w_rw, tm=256)
    pc_ret, pc_rw = _inproj(ctx, csh1, csc1, n1g, w_ret, w_rw, tm=ctx.shape[1])

    cos_t, sin_t = _rope_tables(t)
    ld = jnp.broadcast_to(ret_log_decay[l][:, :, None, None], (2, RET_HEADS, SUBLANES, LANES))
    ret_out = _retention(p_ret, pc_ret, cos_t, sin_t, ld)

    head_id = jnp.arange(RWKV_WIDTH) // RWKV_HEAD_DIM
    seg = (head_id[:, None] == head_id[None, :]).astype(BF16)
    wup = jnp.concatenate([rwkv_w_up[l, 0], rwkv_w_up[l, 1]], axis=-1).astype(BF16)
    aup = jnp.concatenate([rwkv_a_up[l, 0], rwkv_a_up[l, 1]], axis=-1).astype(BF16)
    gup = rwkv_g_up[l].astype(BF16)
    rw_params = (rwkv_shift_mu[l], rwkv_w0[l], wup, rwkv_a0[l], aup, gup,
                 rwkv_k_k[l][None, :], rwkv_k_a[l][None, :], rwkv_r_k[l][None, :], seg)
    phi_c, z_c = _rwkv_terms(pc_rw, *rw_params)[:2]
    zeros = jnp.zeros((bsz, 2, RWKV_PAIRS, LANES, LANES), F32)
    s_ctx = _rwkv_states(phi_c, z_c, zeros)[2]
    phi, z, gf, gb, y0, bonus, gate = _rwkv_terms(p_rw, *rw_params)
    sf, sb, _ = _rwkv_states(phi, z, s_ctx)
    rw_out = _rwkv_out(gf, gb, y0, bonus, gate, sf, sb,
                       rwkv_ln_w[l][None, :], rwkv_ln_b[l][None, :], seg)

    wo = w_out[l].astype(BF16)
    return _mlp(x, ret_out, rw_out, g1, sh2, sc2, g2, norm2_g[l][None, :], final_g[None, :],
                wo[:RET_WIDTH], wo[RET_WIDTH:], w_ff1[l].astype(BF16), b_ff1[l][None, :],
                w_ff2[l].astype(BF16), b_ff2[l][None, :], tm=256)
```

```python
import functools
import math

import jax
import jax.numpy as jnp
from jax import lax
from jax.experimental import pallas as pl
from jax.experimental.pallas import tpu as pltpu

F32 = jnp.float32
BF16 = jnp.bfloat16

GRID_W = 64
RET_HEADS = 4
RET_HEAD_DIM = 128
RET_WIDTH = RET_HEADS * RET_HEAD_DIM
RET_CHUNK = 128
RWKV_HEAD_DIM = 64
RWKV_WIDTH = 512
RWKV_PAIRS = RWKV_WIDTH // (2 * RWKV_HEAD_DIM)
RWKV_CHUNK = 64
RWKV_BATCH_BLOCK = 2
INV_BASE_BLOCK = 16
DECAY_LORA = 64
AAA_LORA = 64
GATE_LORA = 128
SHIFT_COLS = 3 * RWKV_WIDTH + DECAY_LORA + AAA_LORA + GATE_LORA
RET_COLS = 4 * RET_WIDTH
ROPE_BASE = 10000.0
NORM_EPS = 1e-6
GN_EPS = 64e-5
W_DECAY_SCALE = math.exp(-0.5)
LANES = 128
SUBLANES = 8
VMEM_LIMIT = 48 * 1024 * 1024


def _params(*sem):
    return pltpu.CompilerParams(dimension_semantics=sem, vmem_limit_bytes=VMEM_LIMIT)


def _mm(a, b):
    return jnp.dot(a.astype(BF16), b.astype(BF16), preferred_element_type=F32)


def _mm_nt(a, b):
    return lax.dot_general(a.astype(BF16), b.astype(BF16), (((1,), (1,)), ((), ())),
                           preferred_element_type=F32)


def _mm_tn(a, b):
    return lax.dot_general(a.astype(BF16), b.astype(BF16), (((0,), (0,)), ((), ())),
                           preferred_element_type=F32)


def _split3(x):
    hi = x.astype(BF16)
    r1 = x - hi.astype(F32)
    mid = r1.astype(BF16)
    lo = (r1 - mid.astype(F32)).astype(BF16)
    return hi, mid, lo


def _mm_exact_rhs(a_bf16, x):
    hi, mid, lo = _split3(x)
    d = lambda p: jnp.dot(a_bf16, p, preferred_element_type=F32)
    return d(hi) + d(mid) + d(lo)


def _mm_exact_lhs(x, a_bf16):
    hi, mid, lo = _split3(x)
    d = lambda p: jnp.dot(p, a_bf16, preferred_element_type=F32)
    return d(hi) + d(mid) + d(lo)


def _rms(x, g):
    return x * lax.rsqrt(jnp.mean(x * x, axis=-1, keepdims=True) + NORM_EPS) * g


def _ada_kernel(c_ref, w_ref, b_ref, o_ref):
    s = jax.nn.silu(c_ref[...])
    o_ref[...] = jnp.dot(s, w_ref[...], preferred_element_type=F32,
                         precision=lax.Precision.HIGHEST) + b_ref[...]


def _ada(cc, w, b):
    rows, d = cc.shape
    n = w.shape[1]
    tn = n // 4
    return pl.pallas_call(
        _ada_kernel,
        grid=(n // tn,),
        in_specs=[pl.BlockSpec((rows, d), lambda j: (0, 0)),
                  pl.BlockSpec((d, tn), lambda j: (0, j)),
                  pl.BlockSpec((1, tn), lambda j: (0, j))],
        out_specs=pl.BlockSpec((rows, tn), lambda j: (0, j)),
        out_shape=jax.ShapeDtypeStruct((rows, n), F32),
        compiler_params=_params("arbitrary"),
        name="adaln",
    )(cc, w, b)


def _inproj_kernel(x_ref, sh_ref, sc_ref, g_ref, wr_ref, ww_ref, pret_ref, prw_ref):
    h = _rms(x_ref[0], g_ref[...]) * (1.0 + sc_ref[0]) + sh_ref[0]
    hb = h.astype(BF16)
    pret_ref[0] = jnp.dot(hb, wr_ref[...], preferred_element_type=F32)
    prw_ref[0] = jnp.dot(hb, ww_ref[...], preferred_element_type=F32)


def _inproj(x, shift, scale, g, w_ret, w_rw, tm):
    bsz, t, d = x.shape
    const = lambda b, i: (0, 0)
    return pl.pallas_call(
        _inproj_kernel,
        grid=(bsz, t // tm),
        in_specs=[pl.BlockSpec((1, tm, d), lambda b, i: (b, i, 0)),
                  pl.BlockSpec((1, 1, d), lambda b, i: (b, 0, 0)),
                  pl.BlockSpec((1, 1, d), lambda b, i: (b, 0, 0)),
                  pl.BlockSpec((1, d), const),
                  pl.BlockSpec(w_ret.shape, const),
                  pl.BlockSpec(w_rw.shape, const)],
        out_specs=[pl.BlockSpec((1, tm, RET_COLS), lambda b, i: (b, i, 0)),
                   pl.BlockSpec((1, tm, SHIFT_COLS), lambda b, i: (b, i, 0))],
        out_shape=[jax.ShapeDtypeStruct((bsz, t, RET_COLS), F32),
                   jax.ShapeDtypeStruct((bsz, t, SHIFT_COLS), F32)],
        compiler_params=_params("parallel", "parallel"),
        name="inproj",
    )(x, shift, scale, g, w_ret, w_rw)


def _ret_kernel(q_ref, k_ref, v_ref, g_ref, kc_ref, vc_ref, cos_ref, sin_ref, ld_ref, o_ref,
                sf_s, sb_s):
    c = RET_CHUNK
    hd = RET_HEAD_DIM
    t = q_ref.shape[1]
    tc = kc_ref.shape[1]
    n = t // c
    k_scale = hd ** -0.5
    lgf = -jnp.exp(ld_ref[0, 0][0:1, :])
    lgb = -jnp.exp(ld_ref[1, 0][0:1, :])

    jc = lax.broadcasted_iota(jnp.int32, (tc, 1), 0).astype(F32)
    kc = kc_ref[0] * k_scale
    vc = vc_ref[0]
    s_f = _mm_tn(kc * jnp.exp(lgf * (tc - 1.0 - jc)), vc)
    s_b = _mm_tn(kc * jnp.exp(lgb * jc), vc)

    lane = lax.broadcasted_iota(jnp.int32, (c, hd), 1)
    first_half = (lane % 64) < 32

    def rope(x, cs, sn):
        swapped = jnp.where(first_half, pltpu.roll(x, 96, axis=1), pltpu.roll(x, 32, axis=1))
        return x * cs + swapped * sn

    ii = lax.broadcasted_iota(jnp.int32, (c, 1), 0).astype(F32)
    dist = (lax.broadcasted_iota(jnp.int32, (c, c), 0)
            - lax.broadcasted_iota(jnp.int32, (c, c), 1)).astype(F32)
    decay = jnp.where(dist >= 0, jnp.exp(lgf * jnp.maximum(dist, 0.0)),
                      jnp.exp(lgb * jnp.maximum(-dist, 0.0)))
    qdec_f = jnp.exp(lgf * (ii + 1.0))
    kdec_f = jnp.exp(lgf * (c - 1.0 - ii))
    qdec_b = jnp.exp(lgb * (c - ii))
    kdec_b = jnp.exp(lgb * ii)
    cdec_f = jnp.exp(lgf * c)
    cdec_b = jnp.exp(lgb * c)

    qs, ks, kvs = [], [], []
    for ci in range(n):
        sl = slice(ci * c, (ci + 1) * c)
        cs = cos_ref[sl, :]
        sn = sin_ref[sl, :]
        qs.append(rope(q_ref[0, sl, :], cs, sn))
        kx = rope(k_ref[0, sl, :] * k_scale, cs, sn)
        ks.append(kx.astype(BF16))
        kvs.append(_mm_tn(jnp.concatenate([kx * kdec_f, kx * kdec_b], axis=1), v_ref[0, sl, :]))

    for ci in range(n):
        sf_s[ci] = s_f
        s_f = s_f * cdec_f + kvs[ci][:hd]
    for ci in range(n - 1, -1, -1):
        sb_s[ci] = s_b
        s_b = s_b * cdec_b + kvs[ci][hd:]

    for ci in range(n):
        sl = slice(ci * c, (ci + 1) * c)
        qc = qs[ci]
        scores = _mm_nt(qc, ks[ci]) * decay
        lhs = jnp.concatenate([scores, qc * qdec_f, qc * qdec_b], axis=1)
        rhs = jnp.concatenate([v_ref[0, sl, :], sf_s[ci], sb_s[ci]], axis=0)
        o = _mm(lhs, rhs)
        o = o * lax.rsqrt(jnp.mean(o * o, axis=-1, keepdims=True) + NORM_EPS)
        o_ref[0, sl, :] = (o * jax.nn.silu(g_ref[0, sl, :])).astype(o_ref.dtype)


def _retention(p_ret, pc_ret, cos_t, sin_t, ld):
    bsz, t, _ = p_ret.shape
    tc = pc_ret.shape[1]
    hd = RET_HEAD_DIM
    col = lambda j: pl.BlockSpec((1, t, hd), lambda b, h: (b, 0, h + RET_HEADS * j))
    colc = lambda j: pl.BlockSpec((1, tc, hd), lambda b, h: (b, 0, h + RET_HEADS * j))
    n = t // RET_CHUNK
    return pl.pallas_call(
        _ret_kernel,
        grid=(bsz, RET_HEADS),
        in_specs=[col(0), col(1), col(2), col(3), colc(1), colc(2),
                  pl.BlockSpec((t, hd), lambda b, h: (0, 0)),
                  pl.BlockSpec((t, hd), lambda b, h: (0, 0)),
                  pl.BlockSpec((2, 1, SUBLANES, LANES), lambda b, h: (0, h, 0, 0))],
        out_specs=pl.BlockSpec((1, t, hd), lambda b, h: (b, 0, h)),
        out_shape=jax.ShapeDtypeStruct((bsz, t, RET_WIDTH), BF16),
        scratch_shapes=[pltpu.VMEM((n, hd, hd), F32), pltpu.VMEM((n, hd, hd), F32)],
        compiler_params=_params("parallel", "parallel"),
        name="retention",
    )(p_ret, p_ret, p_ret, p_ret, pc_ret, pc_ret, cos_t, sin_t, ld)


def _pair_expand(y, m0):
    zero = jnp.zeros_like(y)
    return jnp.concatenate([jnp.where(m0, y, zero), jnp.where(m0, zero, y)], axis=0)


def _mm_pair(a, y, m0):
    return _mm(a, _pair_expand(y.astype(BF16), m0))


def _rwkv_scan_kernel(pf_ref, pfp_ref, pfn_ref, pb_ref, pbp_ref, pbn_ref,
                      mu_ref, w0_ref, wup_ref, a0_ref, aup_ref, gup_ref, kk_ref, ka_ref, rk_ref,
                      seg_ref, s0_ref, *rest, emit):
    if emit:
        yf_ref, yb_ref, bonus_ref, gate_ref, fin_ref, sh_s, st_s = rest
    else:
        fin_ref, sh_s, st_s = rest
    c = RWKV_CHUNK
    w_ = RWKV_WIDTH
    hd = RWKV_HEAD_DIM
    nb = pf_ref.shape[0]
    j = pl.program_id(1)
    n = pl.num_programs(1)

    @pl.when(j == 0)
    def _():
        st_s[...] = s0_ref[...]

    row = lax.broadcasted_iota(jnp.int32, (c, c), 0)
    colm = lax.broadcasted_iota(jnp.int32, (c, c), 1)
    lane = lax.broadcasted_iota(jnp.int32, (c, LANES), 1)
    trow = lax.broadcasted_iota(jnp.int32, (c, LANES), 0)
    m0 = lane < hd
    jj = lane % hd
    r2 = lax.broadcasted_iota(jnp.int32, (LANES, LANES), 0)
    c2 = lax.broadcasted_iota(jnp.int32, (LANES, LANES), 1)
    blockdiag = (r2 // hd) == (c2 // hd)
    eye_cat = jnp.where(jj == trow, 1.0, 0.0)
    seg = seg_ref[...]

    chains = []
    for bi in range(nb):
        for d in range(2):
            fwd = d == 0
            p_ref, pp_ref, pn_ref = (pf_ref, pfp_ref, pfn_ref) if fwd else (pb_ref, pbp_ref, pbn_ref)
            cidx = j if fwd else n - 1 - j
            slot = 2 * bi + d
            sh_s[slot, 0:SUBLANES, :] = jnp.where(cidx == 0, 0.0, pp_ref[bi])
            sh_s[slot, SUBLANES:SUBLANES + c, :] = p_ref[bi]
            sh_s[slot, SUBLANES + c:2 * SUBLANES + c, :] = jnp.where(cidx == n - 1, 0.0, pn_ref[bi])
            p = p_ref[bi]
            prev = sh_s[slot, SUBLANES - 1:SUBLANES - 1 + c, :]
            nxt = sh_s[slot, SUBLANES + 1:SUBLANES + 1 + c, :]
            ps = p + mu_ref[0:1, :] * (prev - p) + mu_ref[1:2, :] * (nxt - p)

            r = ps[:, 0:w_]
            k = ps[:, w_:2 * w_]
            v = ps[:, 2 * w_:3 * w_]
            wl = ps[:, 3 * w_:3 * w_ + DECAY_LORA]
            al = ps[:, 3 * w_ + DECAY_LORA:3 * w_ + DECAY_LORA + AAA_LORA]

            kk = k * kk_ref[...]
            kk = kk * lax.rsqrt(_mm_exact_lhs(kk * kk, seg) + 1e-12)
            logw = -W_DECAY_SCALE * jax.nn.sigmoid(w0_ref[d:d + 1, :] + _mm(jnp.tanh(wl), wup_ref[d]))
            a = jax.nn.sigmoid(a0_ref[d:d + 1, :] + _mm(al, aup_ref[d]))
            kt = k * (1.0 + (a - 1.0) * ka_ref[...])
            b = kk * a
            if fwd:
                if emit:
                    gl = ps[:, 3 * w_ + DECAY_LORA + AAA_LORA:]
                    gate_ref[bi] = _mm(jax.nn.sigmoid(gl), gup_ref[...])
                    bonus_ref[bi] = _mm_exact_lhs(r * kt * rk_ref[...], seg) * v
                tri = (colm <= row).astype(BF16)
                strict = jj < trow
                omask = jj <= trow
            else:
                tri = (colm >= row).astype(BF16)
                strict = jj > trow
                omask = strict
            big_l = _mm_exact_rhs(tri, logw)
            ltot = big_l[c - 1:c, :] if fwd else big_l[0:1, :]
            e_lx = jnp.exp(big_l - logw)
            e_nl = jnp.exp(-big_l)
            at = -kk * e_lx
            bt = b * e_nl
            ktl = kt * e_nl
            rt = r * (jnp.exp(big_l) if fwd else e_lx)
            e_rest = jnp.exp(ltot - big_l)
            bh = b * e_rest
            kh = kt * e_rest
            e_tot = jnp.exp(ltot)
            for pr in range(RWKV_PAIRS):
                sl = slice(pr * LANES, (pr + 1) * LANES)
                lhs = jnp.concatenate([at[:, sl], rt[:, sl]], axis=0) if emit else at[:, sl]
                m = _mm_nt(lhs, jnp.concatenate([_pair_expand(bt[:, sl].astype(BF16), m0),
                                                 _pair_expand(ktl[:, sl].astype(BF16), m0)], axis=0))
                s_in = st_s[bi, d, pr]
                ch = dict(bi=bi, d=d, pr=pr, sl=sl, v=v[:, sl], bh=bh[:, sl], kh=kh[:, sl],
                          e_tot=e_tot[:, sl], s_in=s_in, sx=_mm_nt(lhs, s_in),
                          a_ab=jnp.where(strict, m[:c, :LANES], 0.0),
                          a_ak=jnp.where(strict, m[:c, LANES:], 0.0))
                if emit:
                    ch.update(m_rb=jnp.where(omask, m[c:, :LANES], 0.0),
                              m_rk=jnp.where(omask, m[c:, LANES:], 0.0))
                chains.append(ch)

    def same_block(size):
        return (trow // size) == (jj // size)

    a_diag = [jnp.where(same_block(INV_BASE_BLOCK), ch["a_ab"], 0.0) for ch in chains]
    tinv = [eye_cat + a for a in a_diag]
    pw = a_diag
    span = 2
    while span < INV_BASE_BLOCK:
        pw = [_mm_pair(p, p, m0) for p in pw]
        tinv = [t + _mm_pair(t, p, m0) for t, p in zip(tinv, pw)]
        span *= 2
    size = INV_BASE_BLOCK
    while size < c:
        off = jnp.logical_and(same_block(2 * size), jnp.logical_not(same_block(size)))
        ta = [_mm_pair(t, jnp.where(off, ch["a_ab"], 0.0), m0) for t, ch in zip(tinv, chains)]
        tinv = [t + _mm_pair(x, t, m0) for t, x in zip(tinv, ta)]
        size *= 2

    if emit:
        avs = [_mm(jnp.concatenate([ch["a_ak"], ch["m_rk"]], axis=0),
                   _pair_expand(ch["v"].astype(BF16), m0)) for ch in chains]
    else:
        avs = [_mm_pair(ch["a_ak"], ch["v"], m0) for ch in chains]
    us = [_mm_pair(t_m, ch["sx"][:c] + av[:c], m0) for ch, t_m, av in zip(chains, tinv, avs)]
    if emit:
        for ch, av, u in zip(chains, avs, us):
            y_ref = yf_ref if ch["d"] == 0 else yb_ref
            y_ref[ch["bi"], :, ch["sl"]] = ch["sx"][c:] + av[c:] + _mm_pair(ch["m_rb"], u, m0)
    upds = [_mm_tn(jnp.concatenate([u, ch["v"]], axis=0), jnp.concatenate([ch["bh"], ch["kh"]], axis=0))
            for ch, u in zip(chains, us)]
    for ch, upd in zip(chains, upds):
        st_s[ch["bi"], ch["d"], ch["pr"]] = ch["s_in"] * ch["e_tot"] + jnp.where(blockdiag, upd, 0.0)

    @pl.when(j == n - 1)
    def _():
        fin_ref[...] = st_s[...]


def _rwkv_scan(p_rw, s0, mu, w0, wup, a0, aup, gup, k_k, k_a, r_k, seg, emit):
    bsz, t, cols = p_rw.shape
    c = RWKV_CHUNK
    nb = RWKV_BATCH_BLOCK
    n = t // c
    blk8 = c // SUBLANES
    last8 = t // SUBLANES - 1
    const2 = lambda b, i: (0, 0)
    const3 = lambda b, i: (0, 0, 0)
    fwd_c = lambda i: i
    bwd_c = lambda i: n - 1 - i

    def views(cmap):
        return [pl.BlockSpec((nb, c, cols), lambda b, i: (b, cmap(i), 0)),
                pl.BlockSpec((nb, SUBLANES, cols), lambda b, i: (b, jnp.maximum(cmap(i) * blk8 - 1, 0), 0)),
                pl.BlockSpec((nb, SUBLANES, cols), lambda b, i: (b, jnp.minimum((cmap(i) + 1) * blk8, last8), 0))]

    state = pl.BlockSpec((nb, 2, RWKV_PAIRS, LANES, LANES), lambda b, i: (b, 0, 0, 0, 0))
    state_shape = jax.ShapeDtypeStruct((bsz, 2, RWKV_PAIRS, LANES, LANES), F32)
    tok_shape = jax.ShapeDtypeStruct((bsz, t, RWKV_WIDTH), F32)
    tok_f = pl.BlockSpec((nb, c, RWKV_WIDTH), lambda b, i: (b, i, 0))
    tok_b = pl.BlockSpec((nb, c, RWKV_WIDTH), lambda b, i: (b, n - 1 - i, 0))
    out_specs = ([tok_f, tok_b, tok_f, tok_f] if emit else []) + [state]
    out_shape = ([tok_shape] * 4 if emit else []) + [state_shape]
    return pl.pallas_call(
        functools.partial(_rwkv_scan_kernel, emit=emit),
        grid=(bsz // nb, n),
        in_specs=views(fwd_c) + views(bwd_c) + [
            pl.BlockSpec(mu.shape, const2), pl.BlockSpec(w0.shape, const2),
            pl.BlockSpec(wup.shape, const3), pl.BlockSpec(a0.shape, const2),
            pl.BlockSpec(aup.shape, const3), pl.BlockSpec(gup.shape, const2),
            pl.BlockSpec(k_k.shape, const2), pl.BlockSpec(k_a.shape, const2),
            pl.BlockSpec(r_k.shape, const2), pl.BlockSpec(seg.shape, const2), state],
        out_specs=out_specs,
        out_shape=out_shape,
        scratch_shapes=[pltpu.VMEM((2 * nb, c + 2 * SUBLANES, cols), F32),
                        pltpu.VMEM((nb, 2, RWKV_PAIRS, LANES, LANES), F32)],
        compiler_params=_params("parallel", "arbitrary"),
        name="rwkv_scan" if emit else "rwkv_ctx_scan",
    )(p_rw, p_rw, p_rw, p_rw, p_rw, p_rw, mu, w0, wup, a0, aup, gup, k_k, k_a, r_k, seg, s0)


def _mlp_kernel(x_ref, ret_ref, yf_ref, yb_ref, bonus_ref, gate_ref, lnw_ref, lnb_ref, seg_ref,
                g1_ref, sh2_ref, sc2_ref, g2_ref, n2g_ref, fg_ref,
                woa_ref, wob_ref, w1_ref, b1_ref, w2_ref, b2_ref, o_ref, *, ff_tile):
    y = yf_ref[0] + yb_ref[0]
    seg = seg_ref[...]
    inv_n = 1.0 / RWKV_HEAD_DIM
    yc = y - _mm_exact_lhs(y, seg) * inv_n
    var = _mm_exact_lhs(yc * yc, seg) * inv_n
    y_n = yc * lax.rsqrt(var + GN_EPS) * lnw_ref[...] + lnb_ref[...]
    rw = ((y_n + bonus_ref[0]) * gate_ref[0]).astype(BF16)

    mix = (jnp.dot(ret_ref[0], woa_ref[...], preferred_element_type=F32)
           + jnp.dot(rw, wob_ref[...], preferred_element_type=F32))
    h1 = x_ref[0] + g1_ref[0] * mix
    n2 = (_rms(h1, n2g_ref[...]) * (1.0 + sc2_ref[0]) + sh2_ref[0]).astype(BF16)
    d_ff = w1_ref.shape[1]
    acc = jnp.zeros(h1.shape, F32)
    for s in range(d_ff // ff_tile):
        sl = slice(s * ff_tile, (s + 1) * ff_tile)
        hid = jnp.dot(n2, w1_ref[:, sl], preferred_element_type=F32) + b1_ref[:, sl]
        hid = jnp.square(jnp.maximum(hid, 0.0)).astype(BF16)
        acc = acc + jnp.dot(hid, w2_ref[sl, :], preferred_element_type=F32)
    h2 = h1 + g2_ref[0] * (acc + b2_ref[...])
    o_ref[0] = _rms(h2, fg_ref[...])


def _mlp(x, ret, yf, yb, bonus, gate, ln_w, ln_b, seg, g1, sh2, sc2, g2, n2g, fg,
         woa, wob, w1, b1, w2, b2, tm):
    bsz, t, d = x.shape
    const = lambda b, i: (0, 0)
    single = pl.Buffered(1)
    mod = pl.BlockSpec((1, 1, d), lambda b, i: (b, 0, 0))
    cw = lambda a: pl.BlockSpec(a.shape, const, pipeline_mode=single)
    tok = lambda w: pl.BlockSpec((1, tm, w), lambda b, i: (b, i, 0))
    return pl.pallas_call(
        functools.partial(_mlp_kernel, ff_tile=1024),
        grid=(bsz, t // tm),
        in_specs=[tok(d), tok(RET_WIDTH), tok(RWKV_WIDTH), tok(RWKV_WIDTH), tok(RWKV_WIDTH),
                  tok(RWKV_WIDTH), cw(ln_w), cw(ln_b), cw(seg),
                  mod, mod, mod, mod, cw(n2g), cw(fg),
                  cw(woa), cw(wob), cw(w1), cw(b1), cw(w2), cw(b2)],
        out_specs=tok(d),
        out_shape=jax.ShapeDtypeStruct((bsz, t, d), F32),
        compiler_params=_params("parallel", "parallel"),
        name="outproj_mlp",
    )(x, ret, yf, yb, bonus, gate, ln_w, ln_b, seg, g1, sh2, sc2, g2, n2g, fg,
      woa, wob, w1, b1, w2, b2)


def _rope_tables(t):
    half = RET_HEAD_DIM // 2
    inv = jnp.power(ROPE_BASE, -jnp.arange(0, half, 2, dtype=F32) / half)
    pos = jnp.arange(t)
    ang_r = (pos // GRID_W).astype(F32)[:, None] * inv[None, :]
    ang_c = (pos % GRID_W).astype(F32)[:, None] * inv[None, :]
    cos_t = jnp.concatenate([jnp.cos(ang_r)] * 2 + [jnp.cos(ang_c)] * 2, axis=-1)
    sin_t = jnp.concatenate([-jnp.sin(ang_r), jnp.sin(ang_r), -jnp.sin(ang_c), jnp.sin(ang_c)], axis=-1)
    return cos_t, sin_t


def kernel(x, c, ctx, c_ctx, w_ada, b_ada, norm1_g, norm2_g, w_in, ret_log_decay, rwkv_shift_mu,
           rwkv_w0, rwkv_w_up, rwkv_a0, rwkv_a_up, rwkv_g_up, rwkv_k_k, rwkv_k_a, rwkv_r_k,
           rwkv_ln_w, rwkv_ln_b, w_out, w_ff1, b_ff1, w_ff2, b_ff2, final_g):
    bsz, t, d = x.shape
    assert w_ada.shape[0] == 1, "single trunk layer"
    assert bsz % RWKV_BATCH_BLOCK == 0
    l = 0

    rows = -(-(bsz + 1) // SUBLANES) * SUBLANES
    cc = jnp.zeros((rows, d), F32).at[:bsz].set(c).at[bsz].set(c_ctx)
    mods = _ada(cc, w_ada[l], b_ada[l][None, :])
    sh1, sc1, g1, sh2, sc2, g2 = [m[:bsz, None, :] for m in jnp.split(mods, 6, axis=-1)]
    csh1, csc1 = [jnp.broadcast_to(m[bsz][None, None, :], (bsz, 1, d))
                  for m in jnp.split(mods, 6, axis=-1)[:2]]

    w_in_b = w_in[l].astype(BF16)
    w_ret, w_rw = w_in_b[:, :RET_COLS], w_in_b[:, RET_COLS:]
    n1g = norm1_g[l][None, :]
    p_ret, p_rw = _inproj(x, sh1, sc1, n1g, w_ret, w_rw, tm=256)
    pc_ret, pc_rw = _inproj(ctx, csh1, csc1, n1g, w_ret, w_rw, tm=ctx.shape[1])

    cos_t, sin_t = _rope_tables(t)
    ld = jnp.broadcast_to(ret_log_decay[l][:, :, None, None], (2, RET_HEADS, SUBLANES, LANES))
    ret_out = _retention(p_ret, pc_ret, cos_t, sin_t, ld)

    head_id = jnp.arange(RWKV_WIDTH) // RWKV_HEAD_DIM
    seg = (head_id[:, None] == head_id[None, :]).astype(BF16)
    rw_params = (rwkv_shift_mu[l], rwkv_w0[l], rwkv_w_up[l].astype(BF16), rwkv_a0[l],
                 rwkv_a_up[l].astype(BF16), rwkv_g_up[l].astype(BF16),
                 rwkv_k_k[l][None, :], rwkv_k_a[l][None, :], rwkv_r_k[l][None, :], seg)
    zeros = jnp.zeros((bsz, 2, RWKV_PAIRS, LANES, LANES), F32)
    (s_ctx,) = _rwkv_scan(pc_rw, zeros, *rw_params, emit=False)
    yf, yb, bonus, gate, _ = _rwkv_scan(p_rw, s_ctx, *rw_params, emit=True)

    wo = w_out[l].astype(BF16)
    return _mlp(x, ret_out, yf, yb, bonus, gate, rwkv_ln_w[l][None, :], rwkv_ln_b[l][None, :], seg,
                g1, sh2, sc2, g2, norm2_g[l][None, :], final_g[None, :],
                wo[:RET_WIDTH], wo[RET_WIDTH:], w_ff1[l].astype(BF16), b_ff1[l][None, :],
                w_ff2[l].astype(BF16), b_ff2[l][None, :], tm=256)
```

```python
import functools
import math

import jax
import jax.numpy as jnp
from jax import lax
from jax.experimental import pallas as pl
from jax.experimental.pallas import tpu as pltpu

F32 = jnp.float32
BF16 = jnp.bfloat16

GRID_W = 64
RET_HEADS = 4
RET_HEAD_DIM = 128
RET_WIDTH = RET_HEADS * RET_HEAD_DIM
RET_CHUNK = 128
RWKV_HEAD_DIM = 64
RWKV_WIDTH = 512
RWKV_PAIRS = RWKV_WIDTH // (2 * RWKV_HEAD_DIM)
RWKV_CHUNK = 64
RWKV_BATCH_BLOCK = 8
RWKV_CHAIN_ROWS = 2
PREP_EVERY_CHAIN_STAGES = 3
INV_BASE_BLOCK = 16
DECAY_LORA = 64
AAA_LORA = 64
GATE_LORA = 128
SHIFT_COLS = 3 * RWKV_WIDTH + DECAY_LORA + AAA_LORA + GATE_LORA
RET_COLS = 4 * RET_WIDTH
ROPE_BASE = 10000.0
NORM_EPS = 1e-6
GN_EPS = 64e-5
W_DECAY_SCALE = math.exp(-0.5)
LANES = 128
SUBLANES = 8
VMEM_LIMIT = 56 * 1024 * 1024


def _params(*sem):
    return pltpu.CompilerParams(dimension_semantics=sem, vmem_limit_bytes=VMEM_LIMIT)


def _mm(a, b):
    return jnp.dot(a.astype(BF16), b.astype(BF16), preferred_element_type=F32)


def _mm_nt(a, b):
    return lax.dot_general(a.astype(BF16), b.astype(BF16), (((1,), (1,)), ((), ())),
                           preferred_element_type=F32)


def _mm_tn(a, b):
    return lax.dot_general(a.astype(BF16), b.astype(BF16), (((0,), (0,)), ((), ())),
                           preferred_element_type=F32)


def _split3(x):
    hi = x.astype(BF16)
    r1 = x - hi.astype(F32)
    mid = r1.astype(BF16)
    lo = (r1 - mid.astype(F32)).astype(BF16)
    return hi, mid, lo


def _mm_exact_rhs(a_bf16, x):
    return jnp.dot(jnp.concatenate([a_bf16] * 3, axis=1), jnp.concatenate(_split3(x), axis=0),
                   preferred_element_type=F32)


def _split2(x):
    hi = x.astype(BF16)
    return hi, (x - hi.astype(F32)).astype(BF16)


def _head_sum(x, red, bcast2):
    n = x.shape[0]
    s = jnp.dot(jnp.concatenate(_split2(x), axis=0), red, preferred_element_type=F32)
    return jnp.dot(jnp.concatenate(_split2(s[:n] + s[n:]), axis=1), bcast2, preferred_element_type=F32)


def _rms(x, g):
    return x * lax.rsqrt(jnp.mean(x * x, axis=-1, keepdims=True) + NORM_EPS) * g


def _ada_kernel(c_ref, w_ref, b_ref, o_ref):
    s = jax.nn.silu(c_ref[...])
    o_ref[...] = jnp.dot(s, w_ref[...], preferred_element_type=F32,
                         precision=lax.Precision.HIGHEST) + b_ref[...]


def _ada(cc, w, b):
    rows, d = cc.shape
    n = w.shape[1]
    tn = n // 4
    return pl.pallas_call(
        _ada_kernel,
        grid=(n // tn,),
        in_specs=[pl.BlockSpec((rows, d), lambda j: (0, 0)),
                  pl.BlockSpec((d, tn), lambda j: (0, j)),
                  pl.BlockSpec((1, tn), lambda j: (0, j))],
        out_specs=pl.BlockSpec((rows, tn), lambda j: (0, j)),
        out_shape=jax.ShapeDtypeStruct((rows, n), F32),
        compiler_params=_params("arbitrary"),
        name="adaln",
    )(cc, w, b)


def _inproj_kernel(x_ref, xp_ref, xn_ref, sh_ref, sc_ref, g_ref, wr_ref, ww_ref, mu_ref,
                   pret_ref, prw_ref, ext_s):
    i = pl.program_id(1)
    n = pl.num_programs(1)
    tm = x_ref.shape[1]
    halo = SUBLANES

    def modulated(xv):
        return _rms(xv, g_ref[...]) * (1.0 + sc_ref[0]) + sh_ref[0]

    h = modulated(x_ref[0])
    pret_ref[0] = jnp.dot(h.astype(BF16), wr_ref[...], preferred_element_type=F32)
    h_ext = jnp.concatenate([modulated(xp_ref[0]), h, modulated(xn_ref[0])], axis=0).astype(BF16)
    p_ext = jnp.dot(h_ext, ww_ref[...], preferred_element_type=F32)
    rid = lax.broadcasted_iota(jnp.int32, (tm + 2 * halo, 1), 0)
    inside = jnp.logical_and(jnp.logical_or(rid >= halo, i > 0),
                             jnp.logical_or(rid < tm + halo, i < n - 1))
    ext_s[...] = jnp.where(inside, p_ext, 0.0)
    p = ext_s[halo:halo + tm, :]
    prev = ext_s[halo - 1:halo - 1 + tm, :]
    nxt = ext_s[halo + 1:halo + 1 + tm, :]
    prw_ref[0] = p + mu_ref[0:1, :] * (prev - p) + mu_ref[1:2, :] * (nxt - p)


def _inproj(x, shift, scale, g, w_ret, w_rw, mu, tm):
    bsz, t, d = x.shape
    const = lambda b, i: (0, 0)
    blk8 = tm // SUBLANES
    last8 = t // SUBLANES - 1
    return pl.pallas_call(
        _inproj_kernel,
        grid=(bsz, t // tm),
        in_specs=[pl.BlockSpec((1, tm, d), lambda b, i: (b, i, 0)),
                  pl.BlockSpec((1, SUBLANES, d), lambda b, i: (b, jnp.maximum(i * blk8 - 1, 0), 0)),
                  pl.BlockSpec((1, SUBLANES, d), lambda b, i: (b, jnp.minimum((i + 1) * blk8, last8), 0)),
                  pl.BlockSpec((1, 1, d), lambda b, i: (b, 0, 0)),
                  pl.BlockSpec((1, 1, d), lambda b, i: (b, 0, 0)),
                  pl.BlockSpec((1, d), const),
                  pl.BlockSpec(w_ret.shape, const),
                  pl.BlockSpec(w_rw.shape, const),
                  pl.BlockSpec(mu.shape, const)],
        out_specs=[pl.BlockSpec((1, tm, RET_COLS), lambda b, i: (b, i, 0)),
                   pl.BlockSpec((1, tm, SHIFT_COLS), lambda b, i: (b, i, 0))],
        out_shape=[jax.ShapeDtypeStruct((bsz, t, RET_COLS), F32),
                   jax.ShapeDtypeStruct((bsz, t, SHIFT_COLS), F32)],
        scratch_shapes=[pltpu.VMEM((tm + 2 * SUBLANES, SHIFT_COLS), F32)],
        compiler_params=_params("parallel", "parallel"),
        name="inproj",
    )(x, x, x, shift, scale, g, w_ret, w_rw, mu)


def _ret_kernel(q_ref, k_ref, v_ref, g_ref, kc_ref, vc_ref, cos_ref, sin_ref, ld_ref, o_ref,
                sf_s, sb_s):
    c = RET_CHUNK
    hd = RET_HEAD_DIM
    t = q_ref.shape[1]
    tc = kc_ref.shape[1]
    n = t // c
    k_scale = hd ** -0.5
    lgf = -jnp.exp(ld_ref[0, 0][0:1, :])
    lgb = -jnp.exp(ld_ref[1, 0][0:1, :])

    jc = lax.broadcasted_iota(jnp.int32, (tc, 1), 0).astype(F32)
    kc = kc_ref[0] * k_scale
    vc = vc_ref[0]
    s_f = _mm_tn(kc * jnp.exp(lgf * (tc - 1.0 - jc)), vc)
    s_b = _mm_tn(kc * jnp.exp(lgb * jc), vc)

    lane = lax.broadcasted_iota(jnp.int32, (c, hd), 1)
    first_half = (lane % 64) < 32

    def rope(x, cs, sn):
        swapped = jnp.where(first_half, pltpu.roll(x, 96, axis=1), pltpu.roll(x, 32, axis=1))
        return x * cs + swapped * sn

    ii = lax.broadcasted_iota(jnp.int32, (c, 1), 0).astype(F32)
    dist = (lax.broadcasted_iota(jnp.int32, (c, c), 0)
            - lax.broadcasted_iota(jnp.int32, (c, c), 1)).astype(F32)
    decay = jnp.where(dist >= 0, jnp.exp(lgf * jnp.maximum(dist, 0.0)),
                      jnp.exp(lgb * jnp.maximum(-dist, 0.0)))
    qdec_f = jnp.exp(lgf * (ii + 1.0))
    kdec_f = jnp.exp(lgf * (c - 1.0 - ii))
    qdec_b = jnp.exp(lgb * (c - ii))
    kdec_b = jnp.exp(lgb * ii)
    cdec_f = jnp.exp(lgf * c)
    cdec_b = jnp.exp(lgb * c)

    qs, ks, kvs = [], [], []
    for ci in range(n):
        sl = slice(ci * c, (ci + 1) * c)
        cs = cos_ref[sl, :]
        sn = sin_ref[sl, :]
        qs.append(rope(q_ref[0, sl, :], cs, sn))
        kx = rope(k_ref[0, sl, :] * k_scale, cs, sn)
        ks.append(kx.astype(BF16))
        kvs.append(_mm_tn(jnp.concatenate([kx * kdec_f, kx * kdec_b], axis=1), v_ref[0, sl, :]))

    for ci in range(n):
        sf_s[ci] = s_f
        s_f = s_f * cdec_f + kvs[ci][:hd]
    for ci in range(n - 1, -1, -1):
        sb_s[ci] = s_b
        s_b = s_b * cdec_b + kvs[ci][hd:]

    for ci in range(n):
        sl = slice(ci * c, (ci + 1) * c)
        qc = qs[ci]
        scores = _mm_nt(qc, ks[ci]) * decay
        lhs = jnp.concatenate([scores, qc * qdec_f, qc * qdec_b], axis=1)
        rhs = jnp.concatenate([v_ref[0, sl, :], sf_s[ci], sb_s[ci]], axis=0)
        o = _mm(lhs, rhs)
        o = o * lax.rsqrt(jnp.mean(o * o, axis=-1, keepdims=True) + NORM_EPS)
        o_ref[0, sl, :] = (o * jax.nn.silu(g_ref[0, sl, :])).astype(o_ref.dtype)


def _retention(p_ret, pc_ret, cos_t, sin_t, ld):
    bsz, t, _ = p_ret.shape
    tc = pc_ret.shape[1]
    hd = RET_HEAD_DIM
    col = lambda j: pl.BlockSpec((1, t, hd), lambda b, h: (b, 0, h + RET_HEADS * j))
    colc = lambda j: pl.BlockSpec((1, tc, hd), lambda b, h: (b, 0, h + RET_HEADS * j))
    n = t // RET_CHUNK
    return pl.pallas_call(
        _ret_kernel,
        grid=(bsz, RET_HEADS),
        in_specs=[col(0), col(1), col(2), col(3), colc(1), colc(2),
                  pl.BlockSpec((t, hd), lambda b, h: (0, 0)),
                  pl.BlockSpec((t, hd), lambda b, h: (0, 0)),
                  pl.BlockSpec((2, 1, SUBLANES, LANES), lambda b, h: (0, h, 0, 0))],
        out_specs=pl.BlockSpec((1, t, hd), lambda b, h: (b, 0, h)),
        out_shape=jax.ShapeDtypeStruct((bsz, t, RET_WIDTH), BF16),
        scratch_shapes=[pltpu.VMEM((n, hd, hd), F32), pltpu.VMEM((n, hd, hd), F32)],
        compiler_params=_params("parallel", "parallel"),
        name="retention",
    )(p_ret, p_ret, p_ret, p_ret, pc_ret, pc_ret, cos_t, sin_t, ld)


def _pair_expand(y, m0):
    zero = jnp.zeros_like(y)
    return jnp.concatenate([jnp.where(m0, y, zero), jnp.where(m0, zero, y)], axis=0)


def _mm_pair(a, y, m0):
    return _mm(a, _pair_expand(y.astype(BF16), m0))


def _rwkv_scan_kernel(pf_ref, pb_ref, w0_ref, wup_ref, a0_ref, aup_ref, gup_ref, kk_ref, ka_ref, rk_ref,
                      red_ref, bc_ref, s0_ref, yf_ref, yb_ref, bonus_ref, gate_ref, fin_ref, st_s):
    c = RWKV_CHUNK
    w_ = RWKV_WIDTH
    hd = RWKV_HEAD_DIM
    nb = pf_ref.shape[0]
    j = pl.program_id(1)
    n = pl.num_programs(1)

    @pl.when(j == 0)
    def _():
        st_s[...] = s0_ref[...]

    row = lax.broadcasted_iota(jnp.int32, (c, c), 0)
    colm = lax.broadcasted_iota(jnp.int32, (c, c), 1)
    lane = lax.broadcasted_iota(jnp.int32, (c, LANES), 1)
    trow = lax.broadcasted_iota(jnp.int32, (c, LANES), 0)
    m0 = lane < hd
    jj = lane % hd
    r2 = lax.broadcasted_iota(jnp.int32, (LANES, LANES), 0)
    c2 = lax.broadcasted_iota(jnp.int32, (LANES, LANES), 1)
    blockdiag = (r2 // hd) == (c2 // hd)
    eye_cat = jnp.where(jj == trow, 1.0, 0.0)
    red = red_ref[...]
    bc2 = bc_ref[...]

    def same_block(size):
        return (trow // size) == (jj // size)

    def stacked(fn, xs):
        out = fn(jnp.concatenate(xs, axis=0))
        return [out[i * c:(i + 1) * c] for i in range(len(xs))]

    def head_sums(xs):
        return stacked(lambda x: jnp.dot(x.astype(BF16), red, preferred_element_type=F32), xs)

    def head_bcast(ss):
        return stacked(lambda s: jnp.dot(jnp.concatenate(_split2(s), axis=1), bc2,
                                         preferred_element_type=F32), ss)

    def prep_stages(rows, out):
        items = [dict(bi=bi, d=d, fwd=d == 0) for bi in rows for d in range(2)]
        for it in items:
            ps = (pf_ref if it["fwd"] else pb_ref)[it["bi"]]
            k = ps[:, w_:2 * w_]
            it.update(ps=ps, k=k, kk0=k * kk_ref[...])
        for it, s in zip(items, head_sums([it["kk0"] * it["kk0"] for it in items])):
            it["nrm"] = s
        for d in range(2):
            sel = [it for it in items if it["d"] == d]
            wls = [jnp.tanh(it["ps"][:, 3 * w_:3 * w_ + DECAY_LORA]) for it in sel]
            als = [it["ps"][:, 3 * w_ + DECAY_LORA:3 * w_ + DECAY_LORA + AAA_LORA] for it in sel]
            for it, wp, ap in zip(sel, stacked(lambda x: _mm(x, wup_ref[d]), wls),
                                  stacked(lambda x: _mm(x, aup_ref[d]), als)):
                it.update(w_pre=wp, a_pre=ap)
        yield
        for it, nb_ in zip(items, head_bcast([lax.rsqrt(it["nrm"] + 1e-12) for it in items])):
            it["inv_norm"] = nb_
        for it in items:
            d = it["d"]
            logw = -W_DECAY_SCALE * jax.nn.sigmoid(w0_ref[d:d + 1, :] + it["w_pre"])
            tri = ((colm <= row) if it["fwd"] else (colm >= row)).astype(BF16)
            it.update(logw=logw, big_l=_mm_exact_rhs(tri, logw),
                      a=jax.nn.sigmoid(a0_ref[d:d + 1, :] + it["a_pre"]))
        yield
        for it in items:
            bi, fwd, ps, k, a, logw, big_l = (it[x] for x in ("bi", "fwd", "ps", "k", "a", "logw", "big_l"))
            r = ps[:, 0:w_]
            v = ps[:, 2 * w_:3 * w_]
            kk = it["kk0"] * it["inv_norm"]
            kt = k * (1.0 + (a - 1.0) * ka_ref[...])
            b = kk * a
            it.update(v=v, rkt=r * kt * rk_ref[...])
            ltot = big_l[c - 1:c, :] if fwd else big_l[0:1, :]
            e_lx = jnp.exp(big_l - logw)
            e_nl = jnp.exp(-big_l)
            e_tot = jnp.exp(ltot)
            e_rest = e_tot * e_nl
            strict = (jj < trow) if fwd else (jj > trow)
            out.append(dict(bi=bi, d=it["d"], v=v, at=-kk * e_lx, bt=b * e_nl, ktl=kt * e_nl,
                            rt=r * (jnp.exp(big_l) if fwd else e_lx), bh=b * e_rest, kh=kt * e_rest,
                            e_tot=e_tot, strict=strict,
                            omask=(jj <= trow) if fwd else strict))
        fw = [it for it in items if it["fwd"]]
        gls = [jax.nn.sigmoid(it["ps"][:, 3 * w_ + DECAY_LORA + AAA_LORA:]) for it in fw]
        for it, g_ in zip(fw, stacked(lambda x: _mm(x, gup_ref[...]), gls)):
            gate_ref[it["bi"]] = g_
        bsums = head_sums([it["rkt"] for it in fw])
        yield
        for it, bb in zip(fw, head_bcast(bsums)):
            bonus_ref[it["bi"]] = bb * it["v"]

    def chain_stages(ops):
        chains = []
        for po in ops:
            bi, d = po["bi"], po["d"]
            for pr in range(RWKV_PAIRS):
                sl = slice(pr * LANES, (pr + 1) * LANES)
                lhs = jnp.concatenate([po["at"][:, sl], po["rt"][:, sl]], axis=0)
                m = _mm_nt(lhs, jnp.concatenate([_pair_expand(po["bt"][:, sl].astype(BF16), m0),
                                                 _pair_expand(po["ktl"][:, sl].astype(BF16), m0)], axis=0))
                s_in = st_s[bi, d, pr]
                ch = dict(bi=bi, d=d, pr=pr, sl=sl, v=po["v"][:, sl], bh=po["bh"][:, sl],
                          kh=po["kh"][:, sl], e_tot=po["e_tot"][:, sl], s_in=s_in, sx=_mm_nt(lhs, s_in),
                          a_ab=jnp.where(po["strict"], m[:c, :LANES], 0.0),
                          a_ak=jnp.where(po["strict"], m[:c, LANES:], 0.0),
                          m_rb=jnp.where(po["omask"], m[c:, :LANES], 0.0),
                          m_rk=jnp.where(po["omask"], m[c:, LANES:], 0.0))
                chains.append(ch)
        yield

        a_diag = [jnp.where(same_block(INV_BASE_BLOCK), ch["a_ab"], 0.0) for ch in chains]
        tinv = [eye_cat + a for a in a_diag]
        pw = a_diag
        span = 2
        while span < INV_BASE_BLOCK:
            pw = [_mm_pair(p, p, m0) for p in pw]
            yield
            tinv = [t + _mm_pair(t, p, m0) for t, p in zip(tinv, pw)]
            yield
            span *= 2
        size = INV_BASE_BLOCK
        while size < c:
            off = jnp.logical_and(same_block(2 * size), jnp.logical_not(same_block(size)))
            ta = [_mm_pair(t, jnp.where(off, ch["a_ab"], 0.0), m0) for t, ch in zip(tinv, chains)]
            yield
            tinv = [t + _mm_pair(x, t, m0) for t, x in zip(tinv, ta)]
            yield
            size *= 2

        avs = [_mm(jnp.concatenate([ch["a_ak"], ch["m_rk"]], axis=0),
                   _pair_expand(ch["v"].astype(BF16), m0)) for ch in chains]
        yield
        us = [_mm_pair(t_m, ch["sx"][:c] + av[:c], m0) for ch, t_m, av in zip(chains, tinv, avs)]
        yield
        for ch, av, u in zip(chains, avs, us):
            y_ref = yf_ref if ch["d"] == 0 else yb_ref
            y_ref[ch["bi"], :, ch["sl"]] = ch["sx"][c:] + av[c:] + _mm_pair(ch["m_rb"], u, m0)
        upds = [_mm_tn(jnp.concatenate([u, ch["v"]], axis=0), jnp.concatenate([ch["bh"], ch["kh"]], axis=0))
                for ch, u in zip(chains, us)]
        yield
        for ch, upd in zip(chains, upds):
            st_s[ch["bi"], ch["d"], ch["pr"]] = ch["s_in"] * ch["e_tot"] + jnp.where(blockdiag, upd, 0.0)

    groups = [range(g, min(g + RWKV_CHAIN_ROWS, nb)) for g in range(0, nb, RWKV_CHAIN_ROWS)]
    ops = []
    for _ in prep_stages(groups[0], ops):
        pass
    for gi in range(len(groups)):
        nxt = []
        prep = prep_stages(groups[gi + 1], nxt) if gi + 1 < len(groups) else iter(())
        for lvl, _ in enumerate(chain_stages(ops)):
            if lvl % PREP_EVERY_CHAIN_STAGES == 0:
                next(prep, None)
        for _ in prep:
            pass
        ops = nxt

    @pl.when(j == n - 1)
    def _():
        fin_ref[...] = st_s[...]


def _rwkv_scan(p_rw, s0, w0, wup, a0, aup, gup, k_k, k_a, r_k, red, bc2):
    bsz, t, cols = p_rw.shape
    c = RWKV_CHUNK
    nb = RWKV_BATCH_BLOCK
    n = t // c
    const2 = lambda b, i: (0, 0)
    const3 = lambda b, i: (0, 0, 0)
    state = pl.BlockSpec((nb, 2, RWKV_PAIRS, LANES, LANES), lambda b, i: (b, 0, 0, 0, 0))
    state_shape = jax.ShapeDtypeStruct((bsz, 2, RWKV_PAIRS, LANES, LANES), F32)
    tok_shape = jax.ShapeDtypeStruct((bsz, t, RWKV_WIDTH), F32)
    tok_f = pl.BlockSpec((nb, c, RWKV_WIDTH), lambda b, i: (b, i, 0))
    tok_b = pl.BlockSpec((nb, c, RWKV_WIDTH), lambda b, i: (b, n - 1 - i, 0))
    return pl.pallas_call(
        _rwkv_scan_kernel,
        grid=(bsz // nb, n),
        in_specs=[
            pl.BlockSpec((nb, c, cols), lambda b, i: (b, i, 0)),
            pl.BlockSpec((nb, c, cols), lambda b, i: (b, n - 1 - i, 0)),
            pl.BlockSpec(w0.shape, const2),
            pl.BlockSpec(wup.shape, const3), pl.BlockSpec(a0.shape, const2),
            pl.BlockSpec(aup.shape, const3), pl.BlockSpec(gup.shape, const2),
            pl.BlockSpec(k_k.shape, const2), pl.BlockSpec(k_a.shape, const2),
            pl.BlockSpec(r_k.shape, const2), pl.BlockSpec(red.shape, const2),
            pl.BlockSpec(bc2.shape, const2), state],
        out_specs=[tok_f, tok_b, tok_f, tok_f, state],
        out_shape=[tok_shape] * 4 + [state_shape],
        scratch_shapes=[pltpu.VMEM((nb, 2, RWKV_PAIRS, LANES, LANES), F32)],
        compiler_params=_params("parallel", "arbitrary"),
        name="rwkv_scan",
    )(p_rw, p_rw, w0, wup, a0, aup, gup, k_k, k_a, r_k, red, bc2, s0)


def _mlp_kernel(x_ref, ret_ref, yf_ref, yb_ref, bonus_ref, gate_ref, lnw_ref, lnb_ref, red_ref, bc_ref,
                g1_ref, sh2_ref, sc2_ref, g2_ref, n2g_ref, fg_ref,
                woa_ref, wob_ref, w1_ref, b1_ref, w2_ref, b2_ref, o_ref, *, ff_tile):
    y = yf_ref[0] + yb_ref[0]
    red = red_ref[...]
    bc2 = bc_ref[...]
    inv_n = 1.0 / RWKV_HEAD_DIM
    yc = y - _head_sum(y, red, bc2) * inv_n
    var = _head_sum(yc * yc, red, bc2) * inv_n
    y_n = yc * lax.rsqrt(var + GN_EPS) * lnw_ref[...] + lnb_ref[...]
    rw = ((y_n + bonus_ref[0]) * gate_ref[0]).astype(BF16)

    mix = (jnp.dot(ret_ref[0], woa_ref[...], preferred_element_type=F32)
           + jnp.dot(rw, wob_ref[...], preferred_element_type=F32))
    h1 = x_ref[0] + g1_ref[0] * mix
    n2 = (_rms(h1, n2g_ref[...]) * (1.0 + sc2_ref[0]) + sh2_ref[0]).astype(BF16)
    d_ff = w1_ref.shape[1]
    acc = jnp.zeros(h1.shape, F32)
    for s in range(d_ff // ff_tile):
        sl = slice(s * ff_tile, (s + 1) * ff_tile)
        hid = jnp.dot(n2, w1_ref[:, sl], preferred_element_type=F32) + b1_ref[:, sl]
        hid = jnp.square(jnp.maximum(hid, 0.0)).astype(BF16)
        acc = acc + jnp.dot(hid, w2_ref[sl, :], preferred_element_type=F32)
    h2 = h1 + g2_ref[0] * (acc + b2_ref[...])
    o_ref[0] = _rms(h2, fg_ref[...])


def _mlp(x, ret, yf, yb, bonus, gate, ln_w, ln_b, red, bc2, g1, sh2, sc2, g2, n2g, fg,
         woa, wob, w1, b1, w2, b2, tm):
    bsz, t, d = x.shape
    const = lambda b, i: (0, 0)
    single = pl.Buffered(1)
    mod = pl.BlockSpec((1, 1, d), lambda b, i: (b, 0, 0))
    cw = lambda a: pl.BlockSpec(a.shape, const, pipeline_mode=single)
    tok = lambda w: pl.BlockSpec((1, tm, w), lambda b, i: (b, i, 0))
    return pl.pallas_call(
        functools.partial(_mlp_kernel, ff_tile=1024),
        grid=(bsz, t // tm),
        in_specs=[tok(d), tok(RET_WIDTH), tok(RWKV_WIDTH), tok(RWKV_WIDTH), tok(RWKV_WIDTH),
                  tok(RWKV_WIDTH), cw(ln_w), cw(ln_b), cw(red), cw(bc2),
                  mod, mod, mod, mod, cw(n2g), cw(fg),
                  cw(woa), cw(wob), cw(w1), cw(b1), cw(w2), cw(b2)],
        out_specs=tok(d),
        out_shape=jax.ShapeDtypeStruct((bsz, t, d), F32),
        compiler_params=_params("parallel", "parallel"),
        name="outproj_mlp",
    )(x, ret, yf, yb, bonus, gate, ln_w, ln_b, red, bc2, g1, sh2, sc2, g2, n2g, fg,
      woa, wob, w1, b1, w2, b2)


def _rope_tables(t):
    half = RET_HEAD_DIM // 2
    inv = jnp.power(ROPE_BASE, -jnp.arange(0, half, 2, dtype=F32) / half)
    pos = jnp.arange(t)
    ang_r = (pos // GRID_W).astype(F32)[:, None] * inv[None, :]
    ang_c = (pos % GRID_W).astype(F32)[:, None] * inv[None, :]
    cos_t = jnp.concatenate([jnp.cos(ang_r)] * 2 + [jnp.cos(ang_c)] * 2, axis=-1)
    sin_t = jnp.concatenate([-jnp.sin(ang_r), jnp.sin(ang_r), -jnp.sin(ang_c), jnp.sin(ang_c)], axis=-1)
    return cos_t, sin_t


def kernel(x, c, ctx, c_ctx, w_ada, b_ada, norm1_g, norm2_g, w_in, ret_log_decay, rwkv_shift_mu,
           rwkv_w0, rwkv_w_up, rwkv_a0, rwkv_a_up, rwkv_g_up, rwkv_k_k, rwkv_k_a, rwkv_r_k,
           rwkv_ln_w, rwkv_ln_b, w_out, w_ff1, b_ff1, w_ff2, b_ff2, final_g):
    bsz, t, d = x.shape
    assert w_ada.shape[0] == 1, "single trunk layer"
    assert bsz % RWKV_BATCH_BLOCK == 0
    l = 0

    rows = -(-(bsz + 1) // SUBLANES) * SUBLANES
    cc = jnp.zeros((rows, d), F32).at[:bsz].set(c).at[bsz].set(c_ctx)
    mods = _ada(cc, w_ada[l], b_ada[l][None, :])
    sh1, sc1, g1, sh2, sc2, g2 = [m[:bsz, None, :] for m in jnp.split(mods, 6, axis=-1)]
    csh1, csc1 = [jnp.broadcast_to(m[bsz][None, None, :], (bsz, 1, d))
                  for m in jnp.split(mods, 6, axis=-1)[:2]]

    w_in_b = w_in[l].astype(BF16)
    w_ret, w_rw = w_in_b[:, :RET_COLS], w_in_b[:, RET_COLS:]
    n1g = norm1_g[l][None, :]
    mu = rwkv_shift_mu[l]
    p_ret, p_rw = _inproj(x, sh1, sc1, n1g, w_ret, w_rw, mu, tm=256)
    pc_ret, pc_rw = _inproj(ctx, csh1, csc1, n1g, w_ret, w_rw, mu, tm=ctx.shape[1])

    cos_t, sin_t = _rope_tables(t)
    ld = jnp.broadcast_to(ret_log_decay[l][:, :, None, None], (2, RET_HEADS, SUBLANES, LANES))
    ret_out = _retention(p_ret, pc_ret, cos_t, sin_t, ld)

    head_id = jnp.arange(RWKV_WIDTH) // RWKV_HEAD_DIM
    red = (head_id[:, None] == jnp.arange(LANES)[None, :]).astype(BF16)
    bc2 = jnp.concatenate([red.T, red.T], axis=0)
    rw_params = (rwkv_w0[l], rwkv_w_up[l].astype(BF16), rwkv_a0[l],
                 rwkv_a_up[l].astype(BF16), rwkv_g_up[l].astype(BF16),
                 rwkv_k_k[l][None, :], rwkv_k_a[l][None, :], rwkv_r_k[l][None, :], red, bc2)
    zeros = jnp.zeros((bsz, 2, RWKV_PAIRS, LANES, LANES), F32)
    s_ctx = _rwkv_scan(pc_rw, zeros, *rw_params)[4]
    yf, yb, bonus, gate, _ = _rwkv_scan(p_rw, s_ctx, *rw_params)

    wo = w_out[l].astype(BF16)
    return _mlp(x, ret_out, yf, yb, bonus, gate, rwkv_ln_w[l][None, :], rwkv_ln_b[l][None, :], red, bc2,
                g1, sh2, sc2, g2, norm2_g[l][None, :], final_g[None, :],
                wo[:RET_WIDTH], wo[RET_WIDTH:], w_ff1[l].astype(BF16), b_ff1[l][None, :],
                w_ff2[l].astype(BF16), b_ff2[l][None, :], tm=256)
```

```python
import functools
import math

import jax
import jax.numpy as jnp
from jax import lax
from jax.experimental import pallas as pl
from jax.experimental.pallas import tpu as pltpu

F32 = jnp.float32
BF16 = jnp.bfloat16

GRID_W = 64
RET_HEADS = 4
RET_HEAD_DIM = 128
RET_WIDTH = RET_HEADS * RET_HEAD_DIM
RET_CHUNK = 128
RWKV_HEAD_DIM = 64
RWKV_WIDTH = 512
RWKV_PAIRS = RWKV_WIDTH // (2 * RWKV_HEAD_DIM)
RWKV_CHUNK = 64
RWKV_BATCH_BLOCK = 8
RWKV_CHAIN_ROWS = 2
PREP_EVERY_CHAIN_STAGES = 3
INV_BASE_BLOCK = 16
DECAY_LORA = 64
AAA_LORA = 64
GATE_LORA = 128
SHIFT_COLS = 3 * RWKV_WIDTH + DECAY_LORA + AAA_LORA + GATE_LORA
RET_COLS = 4 * RET_WIDTH
ROPE_BASE = 10000.0
NORM_EPS = 1e-6
GN_EPS = 64e-5
W_DECAY_SCALE = math.exp(-0.5)
LANES = 128
SUBLANES = 8
TOKEN_TILE = 512
VMEM_LIMIT = 56 * 1024 * 1024


def _params(*sem):
    return pltpu.CompilerParams(dimension_semantics=sem, vmem_limit_bytes=VMEM_LIMIT)


def _mm(a, b):
    return jnp.dot(a.astype(BF16), b.astype(BF16), preferred_element_type=F32)


def _mm_nt(a, b):
    return lax.dot_general(a.astype(BF16), b.astype(BF16), (((1,), (1,)), ((), ())),
                           preferred_element_type=F32)


def _mm_tn(a, b):
    return lax.dot_general(a.astype(BF16), b.astype(BF16), (((0,), (0,)), ((), ())),
                           preferred_element_type=F32)


def _split3(x):
    hi = x.astype(BF16)
    r1 = x - hi.astype(F32)
    mid = r1.astype(BF16)
    lo = (r1 - mid.astype(F32)).astype(BF16)
    return hi, mid, lo


def _mm_exact_rhs(a_bf16, x):
    return jnp.dot(jnp.concatenate([a_bf16] * 3, axis=1), jnp.concatenate(_split3(x), axis=0),
                   preferred_element_type=F32)


def _split2(x):
    hi = x.astype(BF16)
    return hi, (x - hi.astype(F32)).astype(BF16)


def _head_sum(x, red, bcast2):
    n = x.shape[0]
    s = jnp.dot(jnp.concatenate(_split2(x), axis=0), red, preferred_element_type=F32)
    return jnp.dot(jnp.concatenate(_split2(s[:n] + s[n:]), axis=1), bcast2, preferred_element_type=F32)


def _rms(x, g):
    return x * lax.rsqrt(jnp.mean(x * x, axis=-1, keepdims=True) + NORM_EPS) * g


def _ada_kernel(c_ref, w_ref, b_ref, o_ref):
    s = jax.nn.silu(c_ref[...])
    o_ref[...] = jnp.dot(s, w_ref[...], preferred_element_type=F32,
                         precision=lax.Precision.HIGHEST) + b_ref[...]


def _ada(cc, w, b):
    rows, d = cc.shape
    n = w.shape[1]
    tn = n // 4
    return pl.pallas_call(
        _ada_kernel,
        grid=(n // tn,),
        in_specs=[pl.BlockSpec((rows, d), lambda j: (0, 0)),
                  pl.BlockSpec((d, tn), lambda j: (0, j)),
                  pl.BlockSpec((1, tn), lambda j: (0, j))],
        out_specs=pl.BlockSpec((rows, tn), lambda j: (0, j)),
        out_shape=jax.ShapeDtypeStruct((rows, n), F32),
        compiler_params=_params("arbitrary"),
        name="adaln",
    )(cc, w, b)


def _inproj_kernel(x_ref, xp_ref, xn_ref, sh_ref, sc_ref, g_ref, wr_ref, ww_ref, mu_ref,
                   pret_ref, prw_ref, ext_s):
    i = pl.program_id(1)
    n = pl.num_programs(1)
    tm = x_ref.shape[1]
    halo = SUBLANES

    def modulated(xv):
        return _rms(xv, g_ref[...]) * (1.0 + sc_ref[0]) + sh_ref[0]

    h = modulated(x_ref[0])
    pret_ref[0] = jnp.dot(h.astype(BF16), wr_ref[...], preferred_element_type=F32)
    h_ext = jnp.concatenate([modulated(xp_ref[0]), h, modulated(xn_ref[0])], axis=0).astype(BF16)
    p_ext = jnp.dot(h_ext, ww_ref[...], preferred_element_type=F32)
    rid = lax.broadcasted_iota(jnp.int32, (tm + 2 * halo, 1), 0)
    inside = jnp.logical_and(jnp.logical_or(rid >= halo, i > 0),
                             jnp.logical_or(rid < tm + halo, i < n - 1))
    ext_s[...] = jnp.where(inside, p_ext, 0.0)
    p = ext_s[halo:halo + tm, :]
    prev = ext_s[halo - 1:halo - 1 + tm, :]
    nxt = ext_s[halo + 1:halo + 1 + tm, :]
    prw_ref[0] = p + mu_ref[0:1, :] * (prev - p) + mu_ref[1:2, :] * (nxt - p)


def _inproj(x, shift, scale, g, w_ret, w_rw, mu, tm):
    bsz, t, d = x.shape
    const = lambda b, i: (0, 0)
    blk8 = tm // SUBLANES
    last8 = t // SUBLANES - 1
    return pl.pallas_call(
        _inproj_kernel,
        grid=(bsz, t // tm),
        in_specs=[pl.BlockSpec((1, tm, d), lambda b, i: (b, i, 0)),
                  pl.BlockSpec((1, SUBLANES, d), lambda b, i: (b, jnp.maximum(i * blk8 - 1, 0), 0)),
                  pl.BlockSpec((1, SUBLANES, d), lambda b, i: (b, jnp.minimum((i + 1) * blk8, last8), 0)),
                  pl.BlockSpec((1, 1, d), lambda b, i: (b, 0, 0)),
                  pl.BlockSpec((1, 1, d), lambda b, i: (b, 0, 0)),
                  pl.BlockSpec((1, d), const),
                  pl.BlockSpec(w_ret.shape, const, pipeline_mode=pl.Buffered(1)),
                  pl.BlockSpec(w_rw.shape, const, pipeline_mode=pl.Buffered(1)),
                  pl.BlockSpec(mu.shape, const)],
        out_specs=[pl.BlockSpec((1, tm, RET_COLS), lambda b, i: (b, i, 0)),
                   pl.BlockSpec((1, tm, SHIFT_COLS), lambda b, i: (b, i, 0))],
        out_shape=[jax.ShapeDtypeStruct((bsz, t, RET_COLS), F32),
                   jax.ShapeDtypeStruct((bsz, t, SHIFT_COLS), F32)],
        scratch_shapes=[pltpu.VMEM((tm + 2 * SUBLANES, SHIFT_COLS), F32)],
        compiler_params=_params("parallel", "parallel"),
        name="inproj",
    )(x, x, x, shift, scale, g, w_ret, w_rw, mu)


def _ret_kernel(q_ref, k_ref, v_ref, g_ref, kc_ref, vc_ref, cos_ref, sin_ref, ld_ref, o_ref,
                sf_s, sb_s):
    c = RET_CHUNK
    hd = RET_HEAD_DIM
    t = q_ref.shape[1]
    tc = kc_ref.shape[1]
    n = t // c
    k_scale = hd ** -0.5
    lgf = -jnp.exp(ld_ref[0, 0][0:1, :])
    lgb = -jnp.exp(ld_ref[1, 0][0:1, :])

    jc = lax.broadcasted_iota(jnp.int32, (tc, 1), 0).astype(F32)
    kc = kc_ref[0] * k_scale
    vc = vc_ref[0]
    s_f = _mm_tn(kc * jnp.exp(lgf * (tc - 1.0 - jc)), vc)
    s_b = _mm_tn(kc * jnp.exp(lgb * jc), vc)

    lane = lax.broadcasted_iota(jnp.int32, (c, hd), 1)
    first_half = (lane % 64) < 32

    def rope(x, cs, sn):
        swapped = jnp.where(first_half, pltpu.roll(x, 96, axis=1), pltpu.roll(x, 32, axis=1))
        return x * cs + swapped * sn

    ii = lax.broadcasted_iota(jnp.int32, (c, 1), 0).astype(F32)
    dist = (lax.broadcasted_iota(jnp.int32, (c, c), 0)
            - lax.broadcasted_iota(jnp.int32, (c, c), 1)).astype(F32)
    decay = jnp.where(dist >= 0, jnp.exp(lgf * jnp.maximum(dist, 0.0)),
                      jnp.exp(lgb * jnp.maximum(-dist, 0.0)))
    qdec_f = jnp.exp(lgf * (ii + 1.0))
    kdec_f = jnp.exp(lgf * (c - 1.0 - ii))
    qdec_b = jnp.exp(lgb * (c - ii))
    kdec_b = jnp.exp(lgb * ii)
    cdec_f = jnp.exp(lgf * c)
    cdec_b = jnp.exp(lgb * c)

    qs, ks, kvs = [], [], []
    for ci in range(n):
        sl = slice(ci * c, (ci + 1) * c)
        cs = cos_ref[sl, :]
        sn = sin_ref[sl, :]
        qs.append(rope(q_ref[0, sl, :], cs, sn))
        kx = rope(k_ref[0, sl, :] * k_scale, cs, sn)
        ks.append(kx.astype(BF16))
        kvs.append(_mm_tn(jnp.concatenate([kx * kdec_f, kx * kdec_b], axis=1), v_ref[0, sl, :]))

    for ci in range(n):
        sf_s[ci] = s_f
        s_f = s_f * cdec_f + kvs[ci][:hd]
    for ci in range(n - 1, -1, -1):
        sb_s[ci] = s_b
        s_b = s_b * cdec_b + kvs[ci][hd:]

    for ci in range(n):
        sl = slice(ci * c, (ci + 1) * c)
        qc = qs[ci]
        scores = _mm_nt(qc, ks[ci]) * decay
        lhs = jnp.concatenate([scores, qc * qdec_f, qc * qdec_b], axis=1)
        rhs = jnp.concatenate([v_ref[0, sl, :], sf_s[ci], sb_s[ci]], axis=0)
        o = _mm(lhs, rhs)
        o = o * lax.rsqrt(jnp.mean(o * o, axis=-1, keepdims=True) + NORM_EPS)
        o_ref[0, sl, :] = (o * jax.nn.silu(g_ref[0, sl, :])).astype(o_ref.dtype)


def _retention(p_ret, pc_ret, cos_t, sin_t, ld):
    bsz, t, _ = p_ret.shape
    tc = pc_ret.shape[1]
    hd = RET_HEAD_DIM
    col = lambda j: pl.BlockSpec((1, t, hd), lambda b, h: (b, 0, h + RET_HEADS * j))
    colc = lambda j: pl.BlockSpec((1, tc, hd), lambda b, h: (b, 0, h + RET_HEADS * j))
    n = t // RET_CHUNK
    return pl.pallas_call(
        _ret_kernel,
        grid=(bsz, RET_HEADS),
        in_specs=[col(0), col(1), col(2), col(3), colc(1), colc(2),
                  pl.BlockSpec((t, hd), lambda b, h: (0, 0)),
                  pl.BlockSpec((t, hd), lambda b, h: (0, 0)),
                  pl.BlockSpec((2, 1, SUBLANES, LANES), lambda b, h: (0, h, 0, 0))],
        out_specs=pl.BlockSpec((1, t, hd), lambda b, h: (b, 0, h)),
        out_shape=jax.ShapeDtypeStruct((bsz, t, RET_WIDTH), BF16),
        scratch_shapes=[pltpu.VMEM((n, hd, hd), F32), pltpu.VMEM((n, hd, hd), F32)],
        compiler_params=_params("parallel", "parallel"),
        name="retention",
    )(p_ret, p_ret, p_ret, p_ret, pc_ret, pc_ret, cos_t, sin_t, ld)


def _pair_expand(y, m0):
    zero = jnp.zeros_like(y)
    return jnp.concatenate([jnp.where(m0, y, zero), jnp.where(m0, zero, y)], axis=0)


def _mm_pair(a, y, m0):
    return _mm(a, _pair_expand(y.astype(BF16), m0))


def _rwkv_scan_kernel(pf_ref, pb_ref, w0_ref, wup_ref, a0_ref, aup_ref, gup_ref, kk_ref, ka_ref, rk_ref,
                      red_ref, bc_ref, s0_ref, yf_ref, yb_ref, bonus_ref, gate_ref, fin_ref, st_s):
    c = RWKV_CHUNK
    w_ = RWKV_WIDTH
    hd = RWKV_HEAD_DIM
    nb = pf_ref.shape[0]
    j = pl.program_id(1)
    n = pl.num_programs(1)

    @pl.when(j == 0)
    def _():
        st_s[...] = s0_ref[...]

    row = lax.broadcasted_iota(jnp.int32, (c, c), 0)
    colm = lax.broadcasted_iota(jnp.int32, (c, c), 1)
    lane = lax.broadcasted_iota(jnp.int32, (c, LANES), 1)
    trow = lax.broadcasted_iota(jnp.int32, (c, LANES), 0)
    m0 = lane < hd
    jj = lane % hd
    r2 = lax.broadcasted_iota(jnp.int32, (LANES, LANES), 0)
    c2 = lax.broadcasted_iota(jnp.int32, (LANES, LANES), 1)
    blockdiag = (r2 // hd) == (c2 // hd)
    eye_cat = jnp.where(jj == trow, 1.0, 0.0)
    red = red_ref[...]
    bc2 = bc_ref[...]

    def same_block(size):
        return (trow // size) == (jj // size)

    def stacked(fn, xs):
        out = fn(jnp.concatenate(xs, axis=0))
        return [out[i * c:(i + 1) * c] for i in range(len(xs))]

    def head_sums(xs):
        return stacked(lambda x: jnp.dot(x.astype(BF16), red, preferred_element_type=F32), xs)

    def head_bcast(ss):
        return stacked(lambda s: jnp.dot(jnp.concatenate(_split2(s), axis=1), bc2,
                                         preferred_element_type=F32), ss)

    def prep_stages(rows, out):
        items = [dict(bi=bi, d=d, fwd=d == 0) for bi in rows for d in range(2)]
        for it in items:
            ps = (pf_ref if it["fwd"] else pb_ref)[it["bi"]]
            k = ps[:, w_:2 * w_]
            it.update(ps=ps, k=k, kk0=k * kk_ref[...])
        for it, s in zip(items, head_sums([it["kk0"] * it["kk0"] for it in items])):
            it["nrm"] = s
        for d in range(2):
            sel = [it for it in items if it["d"] == d]
            wls = [jnp.tanh(it["ps"][:, 3 * w_:3 * w_ + DECAY_LORA]) for it in sel]
            als = [it["ps"][:, 3 * w_ + DECAY_LORA:3 * w_ + DECAY_LORA + AAA_LORA] for it in sel]
            for it, wp, ap in zip(sel, stacked(lambda x: _mm(x, wup_ref[d]), wls),
                                  stacked(lambda x: _mm(x, aup_ref[d]), als)):
                it.update(w_pre=wp, a_pre=ap)
        yield
        for it, nb_ in zip(items, head_bcast([lax.rsqrt(it["nrm"] + 1e-12) for it in items])):
            it["inv_norm"] = nb_
        for it in items:
            d = it["d"]
            logw = -W_DECAY_SCALE * jax.nn.sigmoid(w0_ref[d:d + 1, :] + it["w_pre"])
            tri = ((colm <= row) if it["fwd"] else (colm >= row)).astype(BF16)
            it.update(logw=logw, big_l=_mm_exact_rhs(tri, logw),
                      a=jax.nn.sigmoid(a0_ref[d:d + 1, :] + it["a_pre"]))
        yield
        for it in items:
            bi, fwd, ps, k, a, logw, big_l = (it[x] for x in ("bi", "fwd", "ps", "k", "a", "logw", "big_l"))
            r = ps[:, 0:w_]
            v = ps[:, 2 * w_:3 * w_]
            kk = it["kk0"] * it["inv_norm"]
            kt = k * (1.0 + (a - 1.0) * ka_ref[...])
            b = kk * a
            it.update(v=v, rkt=r * kt * rk_ref[...])
            ltot = big_l[c - 1:c, :] if fwd else big_l[0:1, :]
            e_lx = jnp.exp(big_l - logw)
            e_nl = jnp.exp(-big_l)
            e_tot = jnp.exp(ltot)
            e_rest = e_tot * e_nl
            strict = (jj < trow) if fwd else (jj > trow)
            out.append(dict(bi=bi, d=it["d"], v=v, at=-kk * e_lx, bt=b * e_nl, ktl=kt * e_nl,
                            rt=r * (jnp.exp(big_l) if fwd else e_lx), bh=b * e_rest, kh=kt * e_rest,
                            e_tot=e_tot, strict=strict,
                            omask=(jj <= trow) if fwd else strict))
        fw = [it for it in items if it["fwd"]]
        gls = [jax.nn.sigmoid(it["ps"][:, 3 * w_ + DECAY_LORA + AAA_LORA:]) for it in fw]
        for it, g_ in zip(fw, stacked(lambda x: _mm(x, gup_ref[...]), gls)):
            gate_ref[it["bi"]] = g_
        bsums = head_sums([it["rkt"] for it in fw])
        yield
        for it, bb in zip(fw, head_bcast(bsums)):
            bonus_ref[it["bi"]] = bb * it["v"]

    def chain_stages(ops):
        chains = []
        for po in ops:
            bi, d = po["bi"], po["d"]
            for pr in range(RWKV_PAIRS):
                sl = slice(pr * LANES, (pr + 1) * LANES)
                lhs = jnp.concatenate([po["at"][:, sl], po["rt"][:, sl]], axis=0)
                m = _mm_nt(lhs, jnp.concatenate([_pair_expand(po["bt"][:, sl].astype(BF16), m0),
                                                 _pair_expand(po["ktl"][:, sl].astype(BF16), m0)], axis=0))
                s_in = st_s[bi, d, pr]
                ch = dict(bi=bi, d=d, pr=pr, sl=sl, v=po["v"][:, sl], bh=po["bh"][:, sl],
                          kh=po["kh"][:, sl], e_tot=po["e_tot"][:, sl], s_in=s_in, sx=_mm_nt(lhs, s_in),
                          a_ab=jnp.where(po["strict"], m[:c, :LANES], 0.0),
                          a_ak=jnp.where(po["strict"], m[:c, LANES:], 0.0),
                          m_rb=jnp.where(po["omask"], m[c:, :LANES], 0.0),
                          m_rk=jnp.where(po["omask"], m[c:, LANES:], 0.0))
                chains.append(ch)
        yield

        a_diag = [jnp.where(same_block(INV_BASE_BLOCK), ch["a_ab"], 0.0) for ch in chains]
        tinv = [eye_cat + a for a in a_diag]
        pw = [_mm_pair(p, p, m0) for p in a_diag]
        yield
        span = 4
        while span < INV_BASE_BLOCK:
            both = [_mm_pair(jnp.concatenate([t, p], axis=0), p, m0) for t, p in zip(tinv, pw)]
            tinv = [t + r[:c] for t, r in zip(tinv, both)]
            pw = [r[c:] for r in both]
            yield
            span *= 2
        tinv = [t + _mm_pair(t, p, m0) for t, p in zip(tinv, pw)]
        yield
        size = INV_BASE_BLOCK
        while size < c:
            off = jnp.logical_and(same_block(2 * size), jnp.logical_not(same_block(size)))
            ta = [_mm_pair(t, jnp.where(off, ch["a_ab"], 0.0), m0) for t, ch in zip(tinv, chains)]
            yield
            tinv = [t + _mm_pair(x, t, m0) for t, x in zip(tinv, ta)]
            yield
            size *= 2

        avs = [_mm(jnp.concatenate([ch["a_ak"], ch["m_rk"]], axis=0),
                   _pair_expand(ch["v"].astype(BF16), m0)) for ch in chains]
        yield
        us = [_mm_pair(t_m, ch["sx"][:c] + av[:c], m0) for ch, t_m, av in zip(chains, tinv, avs)]
        yield
        for ch, av, u in zip(chains, avs, us):
            y_ref = yf_ref if ch["d"] == 0 else yb_ref
            y_ref[ch["bi"], :, ch["sl"]] = ch["sx"][c:] + av[c:] + _mm_pair(ch["m_rb"], u, m0)
        upds = [_mm_tn(jnp.concatenate([u, ch["v"]], axis=0), jnp.concatenate([ch["bh"], ch["kh"]], axis=0))
                for ch, u in zip(chains, us)]
        yield
        for ch, upd in zip(chains, upds):
            st_s[ch["bi"], ch["d"], ch["pr"]] = ch["s_in"] * ch["e_tot"] + jnp.where(blockdiag, upd, 0.0)

    groups = [range(g, min(g + RWKV_CHAIN_ROWS, nb)) for g in range(0, nb, RWKV_CHAIN_ROWS)]
    ops = []
    for _ in prep_stages(groups[0], ops):
        pass
    for gi in range(len(groups)):
        nxt = []
        prep = prep_stages(groups[gi + 1], nxt) if gi + 1 < len(groups) else iter(())
        for lvl, _ in enumerate(chain_stages(ops)):
            if lvl % PREP_EVERY_CHAIN_STAGES == 0:
                next(prep, None)
        for _ in prep:
            pass
        ops = nxt

    @pl.when(j == n - 1)
    def _():
        fin_ref[...] = st_s[...]


def _rwkv_scan(p_rw, s0, w0, wup, a0, aup, gup, k_k, k_a, r_k, red, bc2):
    bsz, t, cols = p_rw.shape
    c = RWKV_CHUNK
    nb = RWKV_BATCH_BLOCK
    n = t // c
    const2 = lambda b, i: (0, 0)
    const3 = lambda b, i: (0, 0, 0)
    state = pl.BlockSpec((nb, 2, RWKV_PAIRS, LANES, LANES), lambda b, i: (b, 0, 0, 0, 0))
    state_shape = jax.ShapeDtypeStruct((bsz, 2, RWKV_PAIRS, LANES, LANES), F32)
    tok_shape = jax.ShapeDtypeStruct((bsz, t, RWKV_WIDTH), F32)
    tok_f = pl.BlockSpec((nb, c, RWKV_WIDTH), lambda b, i: (b, i, 0))
    tok_b = pl.BlockSpec((nb, c, RWKV_WIDTH), lambda b, i: (b, n - 1 - i, 0))
    return pl.pallas_call(
        _rwkv_scan_kernel,
        grid=(bsz // nb, n),
        in_specs=[
            pl.BlockSpec((nb, c, cols), lambda b, i: (b, i, 0)),
            pl.BlockSpec((nb, c, cols), lambda b, i: (b, n - 1 - i, 0)),
            pl.BlockSpec(w0.shape, const2),
            pl.BlockSpec(wup.shape, const3), pl.BlockSpec(a0.shape, const2),
            pl.BlockSpec(aup.shape, const3), pl.BlockSpec(gup.shape, const2),
            pl.BlockSpec(k_k.shape, const2), pl.BlockSpec(k_a.shape, const2),
            pl.BlockSpec(r_k.shape, const2), pl.BlockSpec(red.shape, const2),
            pl.BlockSpec(bc2.shape, const2), state],
        out_specs=[tok_f, tok_b, tok_f, tok_f, state],
        out_shape=[tok_shape] * 4 + [state_shape],
        scratch_shapes=[pltpu.VMEM((nb, 2, RWKV_PAIRS, LANES, LANES), F32)],
        compiler_params=_params("parallel", "arbitrary"),
        name="rwkv_scan",
    )(p_rw, p_rw, w0, wup, a0, aup, gup, k_k, k_a, r_k, red, bc2, s0)


def _mlp_kernel(x_ref, ret_ref, yf_ref, yb_ref, bonus_ref, gate_ref, lnw_ref, lnb_ref, red_ref, bc_ref,
                g1_ref, sh2_ref, sc2_ref, g2_ref, n2g_ref, fg_ref,
                woa_ref, wob_ref, w1_ref, b1_ref, w2_ref, b2_ref, o_ref, *, ff_tile):
    y = yf_ref[0] + yb_ref[0]
    red = red_ref[...]
    bc2 = bc_ref[...]
    inv_n = 1.0 / RWKV_HEAD_DIM
    yc = y - _head_sum(y, red, bc2) * inv_n
    var = _head_sum(yc * yc, red, bc2) * inv_n
    y_n = yc * lax.rsqrt(var + GN_EPS) * lnw_ref[...] + lnb_ref[...]
    rw = ((y_n + bonus_ref[0]) * gate_ref[0]).astype(BF16)

    mix = (jnp.dot(ret_ref[0], woa_ref[...], preferred_element_type=F32)
           + jnp.dot(rw, wob_ref[...], preferred_element_type=F32))
    h1 = x_ref[0] + g1_ref[0] * mix
    n2 = (_rms(h1, n2g_ref[...]) * (1.0 + sc2_ref[0]) + sh2_ref[0]).astype(BF16)
    d_ff = w1_ref.shape[1]
    acc = jnp.zeros(h1.shape, F32)
    for s in range(d_ff // ff_tile):
        sl = slice(s * ff_tile, (s + 1) * ff_tile)
        hid = jnp.dot(n2, w1_ref[:, sl], preferred_element_type=F32) + b1_ref[:, sl]
        hid = jnp.square(jnp.maximum(hid, 0.0)).astype(BF16)
        acc = acc + jnp.dot(hid, w2_ref[sl, :], preferred_element_type=F32)
    h2 = h1 + g2_ref[0] * (acc + b2_ref[...])
    o_ref[0] = _rms(h2, fg_ref[...])


def _mlp(x, ret, yf, yb, bonus, gate, ln_w, ln_b, red, bc2, g1, sh2, sc2, g2, n2g, fg,
         woa, wob, w1, b1, w2, b2, tm):
    bsz, t, d = x.shape
    const = lambda b, i: (0, 0)
    single = pl.Buffered(1)
    mod = pl.BlockSpec((1, 1, d), lambda b, i: (b, 0, 0))
    cw = lambda a: pl.BlockSpec(a.shape, const, pipeline_mode=single)
    tok = lambda w: pl.BlockSpec((1, tm, w), lambda b, i: (b, i, 0))
    return pl.pallas_call(
        functools.partial(_mlp_kernel, ff_tile=1024),
        grid=(bsz, t // tm),
        in_specs=[tok(d), tok(RET_WIDTH), tok(RWKV_WIDTH), tok(RWKV_WIDTH), tok(RWKV_WIDTH),
                  tok(RWKV_WIDTH), cw(ln_w), cw(ln_b), cw(red), cw(bc2),
                  mod, mod, mod, mod, cw(n2g), cw(fg),
                  cw(woa), cw(wob), cw(w1), cw(b1), cw(w2), cw(b2)],
        out_specs=tok(d),
        out_shape=jax.ShapeDtypeStruct((bsz, t, d), F32),
        compiler_params=_params("parallel", "parallel"),
        name="outproj_mlp",
    )(x, ret, yf, yb, bonus, gate, ln_w, ln_b, red, bc2, g1, sh2, sc2, g2, n2g, fg,
      woa, wob, w1, b1, w2, b2)


def _rope_tables(t):
    half = RET_HEAD_DIM // 2
    inv = jnp.power(ROPE_BASE, -jnp.arange(0, half, 2, dtype=F32) / half)
    pos = jnp.arange(t)
    ang_r = (pos // GRID_W).astype(F32)[:, None] * inv[None, :]
    ang_c = (pos % GRID_W).astype(F32)[:, None] * inv[None, :]
    cos_t = jnp.concatenate([jnp.cos(ang_r)] * 2 + [jnp.cos(ang_c)] * 2, axis=-1)
    sin_t = jnp.concatenate([-jnp.sin(ang_r), jnp.sin(ang_r), -jnp.sin(ang_c), jnp.sin(ang_c)], axis=-1)
    return cos_t, sin_t


def kernel(x, c, ctx, c_ctx, w_ada, b_ada, norm1_g, norm2_g, w_in, ret_log_decay, rwkv_shift_mu,
           rwkv_w0, rwkv_w_up, rwkv_a0, rwkv_a_up, rwkv_g_up, rwkv_k_k, rwkv_k_a, rwkv_r_k,
           rwkv_ln_w, rwkv_ln_b, w_out, w_ff1, b_ff1, w_ff2, b_ff2, final_g):
    bsz, t, d = x.shape
    assert w_ada.shape[0] == 1, "single trunk layer"
    assert bsz % RWKV_BATCH_BLOCK == 0
    l = 0

    rows = -(-(bsz + 1) // SUBLANES) * SUBLANES
    cc = jnp.zeros((rows, d), F32).at[:bsz].set(c).at[bsz].set(c_ctx)
    mods = _ada(cc, w_ada[l], b_ada[l][None, :])
    sh1, sc1, g1, sh2, sc2, g2 = [m[:bsz, None, :] for m in jnp.split(mods, 6, axis=-1)]
    csh1, csc1 = [jnp.broadcast_to(m[bsz][None, None, :], (bsz, 1, d))
                  for m in jnp.split(mods, 6, axis=-1)[:2]]

    w_in_b = w_in[l].astype(BF16)
    w_ret, w_rw = w_in_b[:, :RET_COLS], w_in_b[:, RET_COLS:]
    n1g = norm1_g[l][None, :]
    mu = rwkv_shift_mu[l]
    p_ret, p_rw = _inproj(x, sh1, sc1, n1g, w_ret, w_rw, mu, tm=TOKEN_TILE)
    pc_ret, pc_rw = _inproj(ctx, csh1, csc1, n1g, w_ret, w_rw, mu, tm=ctx.shape[1])

    cos_t, sin_t = _rope_tables(t)
    ld = jnp.broadcast_to(ret_log_decay[l][:, :, None, None], (2, RET_HEADS, SUBLANES, LANES))
    ret_out = _retention(p_ret, pc_ret, cos_t, sin_t, ld)

    head_id = jnp.arange(RWKV_WIDTH) // RWKV_HEAD_DIM
    red = (head_id[:, None] == jnp.arange(LANES)[None, :]).astype(BF16)
    bc2 = jnp.concatenate([red.T, red.T], axis=0)
    rw_params = (rwkv_w0[l], rwkv_w_up[l].astype(BF16), rwkv_a0[l],
                 rwkv_a_up[l].astype(BF16), rwkv_g_up[l].astype(BF16),
                 rwkv_k_k[l][None, :], rwkv_k_a[l][None, :], rwkv_r_k[l][None, :], red, bc2)
    zeros = jnp.zeros((bsz, 2, RWKV_PAIRS, LANES, LANES), F32)
    s_ctx = _rwkv_scan(pc_rw, zeros, *rw_params)[4]
    yf, yb, bonus, gate, _ = _rwkv_scan(p_rw, s_ctx, *rw_params)

    wo = w_out[l].astype(BF16)
    return _mlp(x, ret_out, yf, yb, bonus, gate, rwkv_ln_w[l][None, :], rwkv_ln_b[l][None, :], red, bc2,
                g1, sh2, sc2, g2, norm2_g[l][None, :], final_g[None, :],
                wo[:RET_WIDTH], wo[RET_WIDTH:], w_ff1[l].astype(BF16), b_ff1[l][None, :],
                w_ff2[l].astype(BF16), b_ff2[l][None, :], tm=TOKEN_TILE)
```

```python
import functools
import math

import jax
import jax.numpy as jnp
from jax import lax
from jax.experimental import pallas as pl
from jax.experimental.pallas import tpu as pltpu

F32 = jnp.float32
BF16 = jnp.bfloat16

GRID_W = 64
RET_HEADS = 4
RET_HEAD_DIM = 128
RET_WIDTH = RET_HEADS * RET_HEAD_DIM
RET_CHUNK = 128
RWKV_HEAD_DIM = 64
RWKV_WIDTH = 512
RWKV_GROUP_HEADS = 2
RWKV_GROUP_LANES = RWKV_GROUP_HEADS * RWKV_HEAD_DIM
RWKV_PAIRS = RWKV_WIDTH // RWKV_GROUP_LANES
RWKV_STATE_PAIRS = RWKV_WIDTH // 128
RWKV_CHUNK = 64
RWKV_BATCH_BLOCK = 8
RWKV_CHAIN_ROWS = 2
PREP_EVERY_CHAIN_STAGES = 3
INV_BASE_BLOCK = 16
DECAY_LORA = 64
AAA_LORA = 64
GATE_LORA = 128
SHIFT_COLS = 3 * RWKV_WIDTH + DECAY_LORA + AAA_LORA + GATE_LORA
RET_COLS = 4 * RET_WIDTH
ROPE_BASE = 10000.0
NORM_EPS = 1e-6
GN_EPS = 64e-5
W_DECAY_SCALE = math.exp(-0.5)
LANES = 128
SUBLANES = 8
TOKEN_TILE = 512
MERGE_ROW_BLOCK = 128
VMEM_LIMIT = 56 * 1024 * 1024


def _params(*sem):
    return pltpu.CompilerParams(dimension_semantics=sem, vmem_limit_bytes=VMEM_LIMIT)


def _mm(a, b):
    return jnp.dot(a.astype(BF16), b.astype(BF16), preferred_element_type=F32)


def _mm_nt(a, b):
    return lax.dot_general(a.astype(BF16), b.astype(BF16), (((1,), (1,)), ((), ())),
                           preferred_element_type=F32)


def _mm_tn(a, b):
    return lax.dot_general(a.astype(BF16), b.astype(BF16), (((0,), (0,)), ((), ())),
                           preferred_element_type=F32)


def _split3(x):
    hi = x.astype(BF16)
    r1 = x - hi.astype(F32)
    mid = r1.astype(BF16)
    lo = (r1 - mid.astype(F32)).astype(BF16)
    return hi, mid, lo


def _mm_exact_rhs(a_bf16, x):
    return jnp.dot(jnp.concatenate([a_bf16] * 3, axis=1), jnp.concatenate(_split3(x), axis=0),
                   preferred_element_type=F32)


def _split2(x):
    hi = x.astype(BF16)
    return hi, (x - hi.astype(F32)).astype(BF16)


def _rms(x, g):
    return x * lax.rsqrt(jnp.mean(x * x, axis=-1, keepdims=True) + NORM_EPS) * g


def _ada_kernel(c_ref, w_ref, b_ref, o_ref):
    s = jax.nn.silu(c_ref[...])
    o_ref[...] = jnp.dot(s, w_ref[...], preferred_element_type=F32,
                         precision=lax.Precision.HIGHEST) + b_ref[...]


def _ada(cc, w, b):
    rows, d = cc.shape
    n = w.shape[1]
    tn = n // 4
    return pl.pallas_call(
        _ada_kernel,
        grid=(n // tn,),
        in_specs=[pl.BlockSpec((rows, d), lambda j: (0, 0)),
                  pl.BlockSpec((d, tn), lambda j: (0, j)),
                  pl.BlockSpec((1, tn), lambda j: (0, j))],
        out_specs=pl.BlockSpec((rows, tn), lambda j: (0, j)),
        out_shape=jax.ShapeDtypeStruct((rows, n), F32),
        compiler_params=_params("arbitrary"),
        name="adaln",
    )(cc, w, b)


def _inproj_kernel(x_ref, xp_ref, xn_ref, sh_ref, sc_ref, g_ref, wr_ref, ww_ref, mu_ref,
                   pret_ref, prw_ref, ext_s):
    i = pl.program_id(1)
    n = pl.num_programs(1)
    tm = x_ref.shape[1]
    halo = SUBLANES

    def modulated(xv):
        return _rms(xv, g_ref[...]) * (1.0 + sc_ref[0]) + sh_ref[0]

    h = modulated(x_ref[0])
    h_ext = jnp.concatenate([modulated(xp_ref[0]), h, modulated(xn_ref[0])], axis=0).astype(BF16)
    p_ext = jnp.dot(h_ext, ww_ref[...], preferred_element_type=F32)
    rid = lax.broadcasted_iota(jnp.int32, (tm + 2 * halo, 1), 0)
    inside = jnp.logical_and(jnp.logical_or(rid >= halo, i > 0),
                             jnp.logical_or(rid < tm + halo, i < n - 1))
    ext_s[...] = jnp.where(inside, p_ext, 0.0)
    pret_ref[0] = jnp.dot(h.astype(BF16), wr_ref[...], preferred_element_type=F32)
    p = ext_s[halo:halo + tm, :]
    prev = ext_s[halo - 1:halo - 1 + tm, :]
    nxt = ext_s[halo + 1:halo + 1 + tm, :]
    prw_ref[0] = p + mu_ref[0:1, :] * (prev - p) + mu_ref[1:2, :] * (nxt - p)


def _inproj(x, shift, scale, g, w_ret, w_rw, mu, tm):
    bsz, t, d = x.shape
    const = lambda b, i: (0, 0)
    blk8 = tm // SUBLANES
    last8 = t // SUBLANES - 1
    return pl.pallas_call(
        _inproj_kernel,
        grid=(bsz, t // tm),
        in_specs=[pl.BlockSpec((1, tm, d), lambda b, i: (b, i, 0)),
                  pl.BlockSpec((1, SUBLANES, d), lambda b, i: (b, jnp.maximum(i * blk8 - 1, 0), 0)),
                  pl.BlockSpec((1, SUBLANES, d), lambda b, i: (b, jnp.minimum((i + 1) * blk8, last8), 0)),
                  pl.BlockSpec((1, 1, d), lambda b, i: (b, 0, 0)),
                  pl.BlockSpec((1, 1, d), lambda b, i: (b, 0, 0)),
                  pl.BlockSpec((1, d), const),
                  pl.BlockSpec(w_ret.shape, const, pipeline_mode=pl.Buffered(1)),
                  pl.BlockSpec(w_rw.shape, const, pipeline_mode=pl.Buffered(1)),
                  pl.BlockSpec(mu.shape, const)],
        out_specs=[pl.BlockSpec((1, tm, RET_COLS), lambda b, i: (b, i, 0)),
                   pl.BlockSpec((1, tm, SHIFT_COLS), lambda b, i: (b, i, 0))],
        out_shape=[jax.ShapeDtypeStruct((bsz, t, RET_COLS), F32),
                   jax.ShapeDtypeStruct((bsz, t, SHIFT_COLS), F32)],
        scratch_shapes=[pltpu.VMEM((tm + 2 * SUBLANES, SHIFT_COLS), F32)],
        compiler_params=_params("parallel", "parallel"),
        name="inproj",
    )(x, x, x, shift, scale, g, w_ret, w_rw, mu)


def _ret_kernel(q_ref, k_ref, v_ref, g_ref, kc_ref, vc_ref, cos_ref, sin_ref, ld_ref, o_ref):
    c = RET_CHUNK
    hd = RET_HEAD_DIM
    t = q_ref.shape[1]
    tc = kc_ref.shape[1]
    n = t // c
    k_scale = hd ** -0.5
    lgf = -jnp.exp(ld_ref[0, 0][0:1, :])
    lgb = -jnp.exp(ld_ref[1, 0][0:1, :])

    jc = lax.broadcasted_iota(jnp.int32, (tc, 1), 0).astype(F32)
    kc = kc_ref[0] * k_scale
    vc = vc_ref[0]
    s_f = _mm_tn(kc * jnp.exp(lgf * (tc - 1.0 - jc)), vc)
    s_b = _mm_tn(kc * jnp.exp(lgb * jc), vc)

    lane = lax.broadcasted_iota(jnp.int32, (c, hd), 1)
    first_half = (lane % 64) < 32

    def rope(x, cs, sn):
        swapped = jnp.where(first_half, pltpu.roll(x, 96, axis=1), pltpu.roll(x, 32, axis=1))
        return x * cs + swapped * sn

    ii = lax.broadcasted_iota(jnp.int32, (c, 1), 0).astype(F32)
    dist = (lax.broadcasted_iota(jnp.int32, (c, c), 0)
            - lax.broadcasted_iota(jnp.int32, (c, c), 1)).astype(F32)
    decay = jnp.where(dist >= 0, jnp.exp(lgf * jnp.maximum(dist, 0.0)),
                      jnp.exp(lgb * jnp.maximum(-dist, 0.0)))
    qdec_f = jnp.exp(lgf * (ii + 1.0))
    kdec_f = jnp.exp(lgf * (c - 1.0 - ii))
    qdec_b = jnp.exp(lgb * (c - ii))
    kdec_b = jnp.exp(lgb * ii)
    cdec_f = jnp.exp(lgf * c)
    cdec_b = jnp.exp(lgb * c)

    chunks = [slice(ci * c, (ci + 1) * c) for ci in range(n)]
    qs = [rope(q_ref[0, sl, :], cos_ref[sl, :], sin_ref[sl, :]) for sl in chunks]
    ks = [rope(k_ref[0, sl, :] * k_scale, cos_ref[sl, :], sin_ref[sl, :]) for sl in chunks]
    scores = [_mm_nt(qc, kx) for qc, kx in zip(qs, ks)]
    kvs = [_mm_tn(jnp.concatenate([kx * kdec_f, kx * kdec_b], axis=1), v_ref[0, sl, :])
           for kx, sl in zip(ks, chunks)]

    sfs, sbs = [None] * n, [None] * n
    for ci in range(n):
        sfs[ci] = s_f
        s_f = s_f * cdec_f + kvs[ci][:hd]
    for ci in range(n - 1, -1, -1):
        sbs[ci] = s_b
        s_b = s_b * cdec_b + kvs[ci][hd:]

    outs = [_mm(jnp.concatenate([sc * decay, qc * qdec_f, qc * qdec_b], axis=1),
                jnp.concatenate([v_ref[0, sl, :].astype(BF16), sf.astype(BF16), sb.astype(BF16)], axis=0))
            for sc, qc, sl, sf, sb in zip(scores, qs, chunks, sfs, sbs)]
    for o, sl in zip(outs, chunks):
        o = o * lax.rsqrt(jnp.mean(o * o, axis=-1, keepdims=True) + NORM_EPS)
        o_ref[0, sl, :] = (o * jax.nn.silu(g_ref[0, sl, :])).astype(o_ref.dtype)


def _retention(p_ret, pc_ret, cos_t, sin_t, ld):
    bsz, t, _ = p_ret.shape
    tc = pc_ret.shape[1]
    hd = RET_HEAD_DIM
    col = lambda j: pl.BlockSpec((1, t, hd), lambda b, h: (b, 0, h + RET_HEADS * j))
    colc = lambda j: pl.BlockSpec((1, tc, hd), lambda b, h: (b, 0, h + RET_HEADS * j))
    return pl.pallas_call(
        _ret_kernel,
        grid=(bsz, RET_HEADS),
        in_specs=[col(0), col(1), col(2), col(3), colc(1), colc(2),
                  pl.BlockSpec((t, hd), lambda b, h: (0, 0)),
                  pl.BlockSpec((t, hd), lambda b, h: (0, 0)),
                  pl.BlockSpec((2, 1, SUBLANES, LANES), lambda b, h: (0, h, 0, 0))],
        out_specs=pl.BlockSpec((1, t, hd), lambda b, h: (b, 0, h)),
        out_shape=jax.ShapeDtypeStruct((bsz, t, RET_WIDTH), BF16),
        compiler_params=_params("parallel", "parallel"),
        name="retention",
    )(p_ret, p_ret, p_ret, p_ret, pc_ret, pc_ret, cos_t, sin_t, ld)


def _pair_expand(y, head_of_lane):
    zero = jnp.zeros_like(y)
    return jnp.concatenate([jnp.where(head_of_lane == h, y, zero) for h in range(RWKV_GROUP_HEADS)],
                           axis=0)


def _mm_pair(a, y, head_of_lane):
    return _mm(a, _pair_expand(y.astype(BF16), head_of_lane))


def _rwkv_scan_kernel(pf_ref, pb_ref, w0_ref, wup_ref, a0_ref, aup_ref, gup_ref, kk_ref, ka_ref, rk_ref,
                      red_ref, bc_ref, s0_ref, yf_ref, yb_ref, bonus_ref, gate_ref, fin_ref, st_s):
    c = RWKV_CHUNK
    w_ = RWKV_WIDTH
    hd = RWKV_HEAD_DIM
    nb = pf_ref.shape[0]
    j = pl.program_id(1)
    n = pl.num_programs(1)

    per_group = RWKV_GROUP_LANES // LANES

    def pair_window(ref, p):
        g, o = divmod(p, per_group)
        return ref.at[:, :, g, o * LANES:(o + 1) * LANES, o * LANES:(o + 1) * LANES]

    @pl.when(j == 0)
    def _():
        st_s[...] = jnp.zeros(st_s.shape, F32)
        for p in range(s0_ref.shape[2]):
            pair_window(st_s, p)[...] = s0_ref[:, :, p]

    row = lax.broadcasted_iota(jnp.int32, (c, c), 0)
    colm = lax.broadcasted_iota(jnp.int32, (c, c), 1)
    gl = RWKV_GROUP_LANES
    lane = lax.broadcasted_iota(jnp.int32, (c, gl), 1)
    trow = lax.broadcasted_iota(jnp.int32, (c, gl), 0)
    m0 = lane // hd
    jj = lane % hd
    r2 = lax.broadcasted_iota(jnp.int32, (gl, gl), 0)
    c2 = lax.broadcasted_iota(jnp.int32, (gl, gl), 1)
    blockdiag = (r2 // hd) == (c2 // hd)
    eye_cat = jnp.where(jj == trow, 1.0, 0.0)
    red = red_ref[...]
    bc2 = bc_ref[...]

    def same_block(size):
        return (trow // size) == (jj // size)

    def stacked(fn, xs):
        out = fn(jnp.concatenate(xs, axis=0))
        return [out[i * c:(i + 1) * c] for i in range(len(xs))]

    def head_sums(xs):
        return stacked(lambda x: jnp.dot(x.astype(BF16), red, preferred_element_type=F32), xs)

    def head_bcast(ss):
        return stacked(lambda s: jnp.dot(jnp.concatenate(_split2(s), axis=1), bc2,
                                         preferred_element_type=F32), ss)

    def prep_stages(rows, out):
        items = [dict(bi=bi, d=d, fwd=d == 0) for bi in rows for d in range(2)]
        for it in items:
            ps = (pf_ref if it["fwd"] else pb_ref)[it["bi"]]
            k = ps[:, w_:2 * w_]
            it.update(ps=ps, k=k, kk0=k * kk_ref[...])
        for it, s in zip(items, head_sums([it["kk0"] * it["kk0"] for it in items])):
            it["nrm"] = s
        for d in range(2):
            sel = [it for it in items if it["d"] == d]
            wls = [jnp.tanh(it["ps"][:, 3 * w_:3 * w_ + DECAY_LORA]) for it in sel]
            als = [it["ps"][:, 3 * w_ + DECAY_LORA:3 * w_ + DECAY_LORA + AAA_LORA] for it in sel]
            for it, wp, ap in zip(sel, stacked(lambda x: _mm(x, wup_ref[d]), wls),
                                  stacked(lambda x: _mm(x, aup_ref[d]), als)):
                it.update(w_pre=wp, a_pre=ap)
        yield
        for it, nb_ in zip(items, head_bcast([lax.rsqrt(it["nrm"] + 1e-12) for it in items])):
            it["inv_norm"] = nb_
        for it in items:
            d = it["d"]
            logw = -W_DECAY_SCALE * jax.nn.sigmoid(w0_ref[d:d + 1, :] + it["w_pre"])
            tri = ((colm <= row) if it["fwd"] else (colm >= row)).astype(BF16)
            it.update(logw=logw, big_l=_mm_exact_rhs(tri, logw),
                      a=jax.nn.sigmoid(a0_ref[d:d + 1, :] + it["a_pre"]))
        yield
        for it in items:
            bi, fwd, ps, k, a, logw, big_l = (it[x] for x in ("bi", "fwd", "ps", "k", "a", "logw", "big_l"))
            r = ps[:, 0:w_]
            v = ps[:, 2 * w_:3 * w_]
            kk = it["kk0"] * it["inv_norm"]
            kt = k * (1.0 + (a - 1.0) * ka_ref[...])
            b = kk * a
            it.update(v=v, rkt=r * kt * rk_ref[...])
            ltot = big_l[c - 1:c, :] if fwd else big_l[0:1, :]
            e_lx = jnp.exp(big_l - logw)
            e_nl = jnp.exp(-big_l)
            e_tot = jnp.exp(ltot)
            e_rest = e_tot * e_nl
            strict = (jj < trow) if fwd else (jj > trow)
            out.append(dict(bi=bi, d=it["d"], v=v, at=-kk * e_lx, bt=b * e_nl, ktl=kt * e_nl,
                            rt=r * (jnp.exp(big_l) if fwd else e_lx), bh=b * e_rest, kh=kt * e_rest,
                            e_tot=e_tot, strict=strict,
                            omask=(jj <= trow) if fwd else strict))
        fw = [it for it in items if it["fwd"]]
        gls = [jax.nn.sigmoid(it["ps"][:, 3 * w_ + DECAY_LORA + AAA_LORA:]) for it in fw]
        for it, g_ in zip(fw, stacked(lambda x: _mm(x, gup_ref[...]), gls)):
            gate_ref[it["bi"]] = g_
        bsums = head_sums([it["rkt"] for it in fw])
        yield
        for it, bb in zip(fw, head_bcast(bsums)):
            bonus_ref[it["bi"]] = bb * it["v"]

    def chain_stages(ops):
        chains = []
        for po in ops:
            bi, d = po["bi"], po["d"]
            for pr in range(RWKV_PAIRS):
                sl = slice(pr * gl, (pr + 1) * gl)
                lhs = jnp.concatenate([po["at"][:, sl], po["rt"][:, sl]], axis=0)
                m = _mm_nt(lhs, jnp.concatenate([_pair_expand(po["bt"][:, sl].astype(BF16), m0),
                                                 _pair_expand(po["ktl"][:, sl].astype(BF16), m0)], axis=0))
                s_in = st_s[bi, d, pr]
                ch = dict(bi=bi, d=d, pr=pr, sl=sl, v=po["v"][:, sl], bh=po["bh"][:, sl],
                          kh=po["kh"][:, sl], e_tot=po["e_tot"][:, sl], s_in=s_in, sx=_mm_nt(lhs, s_in),
                          a_ab=jnp.where(po["strict"], m[:c, :gl], 0.0),
                          a_ak=jnp.where(po["strict"], m[:c, gl:], 0.0),
                          m_rb=jnp.where(po["omask"], m[c:, :gl], 0.0),
                          m_rk=jnp.where(po["omask"], m[c:, gl:], 0.0))
                chains.append(ch)
        yield

        a_diag = [jnp.where(same_block(INV_BASE_BLOCK), ch["a_ab"], 0.0) for ch in chains]
        tinv = [eye_cat + a for a in a_diag]
        pw = [_mm_pair(p, p, m0) for p in a_diag]
        yield
        span = 4
        while span < INV_BASE_BLOCK:
            both = [_mm_pair(jnp.concatenate([t, p], axis=0), p, m0) for t, p in zip(tinv, pw)]
            tinv = [t + r[:c] for t, r in zip(tinv, both)]
            pw = [r[c:] for r in both]
            yield
            span *= 2
        tinv = [t + _mm_pair(t, p, m0) for t, p in zip(tinv, pw)]
        yield
        size = INV_BASE_BLOCK
        while size < c:
            off = jnp.logical_and(same_block(2 * size), jnp.logical_not(same_block(size)))
            ta = [_mm_pair(t, jnp.where(off, ch["a_ab"], 0.0), m0) for t, ch in zip(tinv, chains)]
            yield
            tinv = [t + _mm_pair(x, t, m0) for t, x in zip(tinv, ta)]
            yield
            size *= 2

        avs = [_mm(jnp.concatenate([ch["a_ak"], ch["m_rk"]], axis=0),
                   _pair_expand(ch["v"].astype(BF16), m0)) for ch in chains]
        yield
        us = [_mm_pair(t_m, ch["sx"][:c] + av[:c], m0) for ch, t_m, av in zip(chains, tinv, avs)]
        yield
        for ch, av, u in zip(chains, avs, us):
            y_ref = yf_ref if ch["d"] == 0 else yb_ref
            y_ref[ch["bi"], :, ch["sl"]] = ch["sx"][c:] + av[c:] + _mm_pair(ch["m_rb"], u, m0)
        upds = [_mm_tn(jnp.concatenate([u, ch["v"]], axis=0), jnp.concatenate([ch["bh"], ch["kh"]], axis=0))
                for ch, u in zip(chains, us)]
        yield
        for ch, upd in zip(chains, upds):
            st_s[ch["bi"], ch["d"], ch["pr"]] = ch["s_in"] * ch["e_tot"] + jnp.where(blockdiag, upd, 0.0)

    groups = [range(g, min(g + RWKV_CHAIN_ROWS, nb)) for g in range(0, nb, RWKV_CHAIN_ROWS)]
    ops = []
    for _ in prep_stages(groups[0], ops):
        pass
    for gi in range(len(groups)):
        nxt = []
        prep = prep_stages(groups[gi + 1], nxt) if gi + 1 < len(groups) else iter(())
        for lvl, _ in enumerate(chain_stages(ops)):
            if lvl % PREP_EVERY_CHAIN_STAGES == 0:
                next(prep, None)
        for _ in prep:
            pass
        ops = nxt

    @pl.when(j == n - 1)
    def _():
        for p in range(fin_ref.shape[2]):
            fin_ref[:, :, p] = pair_window(st_s, p)[...]


def _rwkv_scan(p_rw, s0, w0, wup, a0, aup, gup, k_k, k_a, r_k, red, bc2):
    bsz, t, cols = p_rw.shape
    c = RWKV_CHUNK
    nb = RWKV_BATCH_BLOCK
    n = t // c
    const2 = lambda b, i: (0, 0)
    const3 = lambda b, i: (0, 0, 0)
    state_blk = (nb, 2, RWKV_STATE_PAIRS, LANES, LANES)
    state_map = lambda b, i: (b, 0, 0, 0, 0)
    state_shape = jax.ShapeDtypeStruct((bsz, 2, RWKV_STATE_PAIRS, LANES, LANES), F32)
    tok_shape = jax.ShapeDtypeStruct((bsz, t, RWKV_WIDTH), F32)
    tok_f = pl.BlockSpec((nb, c, RWKV_WIDTH), lambda b, i: (b, i, 0))
    tok_b = pl.BlockSpec((nb, c, RWKV_WIDTH), lambda b, i: (b, n - 1 - i, 0))
    return pl.pallas_call(
        _rwkv_scan_kernel,
        grid=(bsz // nb, n),
        in_specs=[
            pl.BlockSpec((nb, c, cols), lambda b, i: (b, i, 0)),
            pl.BlockSpec((nb, c, cols), lambda b, i: (b, n - 1 - i, 0)),
            pl.BlockSpec(w0.shape, const2),
            pl.BlockSpec(wup.shape, const3), pl.BlockSpec(a0.shape, const2),
            pl.BlockSpec(aup.shape, const3), pl.BlockSpec(gup.shape, const2),
            pl.BlockSpec(k_k.shape, const2), pl.BlockSpec(k_a.shape, const2),
            pl.BlockSpec(r_k.shape, const2), pl.BlockSpec(red.shape, const2),
            pl.BlockSpec(bc2.shape, const2),
            pl.BlockSpec(state_blk, state_map, pipeline_mode=pl.Buffered(1))],
        out_specs=[tok_f, tok_b, tok_f, tok_f, pl.BlockSpec(state_blk, state_map)],
        out_shape=[tok_shape] * 4 + [state_shape],
        scratch_shapes=[pltpu.VMEM((nb, 2, RWKV_PAIRS, RWKV_GROUP_LANES, RWKV_GROUP_LANES), F32)],
        compiler_params=_params("parallel", "arbitrary"),
        name="rwkv_scan",
    )(p_rw, p_rw, w0, wup, a0, aup, gup, k_k, k_a, r_k, red, bc2, s0)


def _mlp_kernel(x_ref, ret_ref, yf_ref, yb_ref, bonus_ref, gate_ref, lnw_ref, lnb_ref, red_ref, bc_ref,
                g1_ref, sh2_ref, sc2_ref, g2_ref, n2g_ref, fg_ref,
                woa_ref, wob_ref, w1_ref, b1_ref, w2_ref, b2_ref, o_ref, *, ff_tile):
    mix_ret = jnp.dot(ret_ref[0], woa_ref[...], preferred_element_type=F32)
    red = red_ref[...]
    bc2 = bc_ref[...]
    inv_n = 1.0 / RWKV_HEAD_DIM
    blk = MERGE_ROW_BLOCK
    rows = [slice(r, r + blk) for r in range(0, x_ref.shape[1], blk)]

    def head_means(xs):
        ss = [jnp.dot(jnp.concatenate(_split2(x), axis=0), red, preferred_element_type=F32) for x in xs]
        return [jnp.dot(jnp.concatenate(_split2(s[:blk] + s[blk:]), axis=1), bc2,
                        preferred_element_type=F32) * inv_n for s in ss]

    ys = [yf_ref[0, r, :] + yb_ref[0, r, :] for r in rows]
    ycs = [y - m for y, m in zip(ys, head_means(ys))]
    variances = head_means([yc * yc for yc in ycs])
    rw = jnp.concatenate(
        [((yc * lax.rsqrt(var + GN_EPS) * lnw_ref[...] + lnb_ref[...] + bonus_ref[0, r, :])
          * gate_ref[0, r, :]).astype(BF16) for yc, var, r in zip(ycs, variances, rows)], axis=0)
    mix = mix_ret + jnp.dot(rw, wob_ref[...], preferred_element_type=F32)
    h1 = x_ref[0] + g1_ref[0] * mix
    n2 = (_rms(h1, n2g_ref[...]) * (1.0 + sc2_ref[0]) + sh2_ref[0]).astype(BF16)
    d_ff = w1_ref.shape[1]
    slices = [slice(s * ff_tile, (s + 1) * ff_tile) for s in range(d_ff // ff_tile)]

    def up(sl):
        return jnp.dot(n2, w1_ref[:, sl], preferred_element_type=F32) + b1_ref[:, sl]

    acc = jnp.zeros(h1.shape, F32)
    hid = up(slices[0])
    for s, sl in enumerate(slices):
        nxt = up(slices[s + 1]) if s + 1 < len(slices) else None
        act = jnp.square(jnp.maximum(hid, 0.0)).astype(BF16)
        acc = acc + jnp.dot(act, w2_ref[sl, :], preferred_element_type=F32)
        hid = nxt
    h2 = h1 + g2_ref[0] * (acc + b2_ref[...])
    o_ref[0] = _rms(h2, fg_ref[...])


def _mlp(x, ret, yf, yb, bonus, gate, ln_w, ln_b, red, bc2, g1, sh2, sc2, g2, n2g, fg,
         woa, wob, w1, b1, w2, b2, tm):
    bsz, t, d = x.shape
    const = lambda b, i: (0, 0)
    single = pl.Buffered(1)
    mod = pl.BlockSpec((1, 1, d), lambda b, i: (b, 0, 0))
    cw = lambda a: pl.BlockSpec(a.shape, const, pipeline_mode=single)
    tok = lambda w: pl.BlockSpec((1, tm, w), lambda b, i: (b, i, 0))
    return pl.pallas_call(
        functools.partial(_mlp_kernel, ff_tile=1024),
        grid=(bsz, t // tm),
        in_specs=[tok(d), tok(RET_WIDTH), tok(RWKV_WIDTH), tok(RWKV_WIDTH), tok(RWKV_WIDTH),
                  tok(RWKV_WIDTH), cw(ln_w), cw(ln_b), cw(red), cw(bc2),
                  mod, mod, mod, mod, cw(n2g), cw(fg),
                  cw(woa), cw(wob), cw(w1), cw(b1), cw(w2), cw(b2)],
        out_specs=tok(d),
        out_shape=jax.ShapeDtypeStruct((bsz, t, d), F32),
        compiler_params=_params("parallel", "parallel"),
        name="outproj_mlp",
    )(x, ret, yf, yb, bonus, gate, ln_w, ln_b, red, bc2, g1, sh2, sc2, g2, n2g, fg,
      woa, wob, w1, b1, w2, b2)


def _rope_tables(t):
    half = RET_HEAD_DIM // 2
    inv = jnp.power(ROPE_BASE, -jnp.arange(0, half, 2, dtype=F32) / half)
    pos = jnp.arange(t)
    ang_r = (pos // GRID_W).astype(F32)[:, None] * inv[None, :]
    ang_c = (pos % GRID_W).astype(F32)[:, None] * inv[None, :]
    cos_t = jnp.concatenate([jnp.cos(ang_r)] * 2 + [jnp.cos(ang_c)] * 2, axis=-1)
    sin_t = jnp.concatenate([-jnp.sin(ang_r), jnp.sin(ang_r), -jnp.sin(ang_c), jnp.sin(ang_c)], axis=-1)
    return cos_t, sin_t


def kernel(x, c, ctx, c_ctx, w_ada, b_ada, norm1_g, norm2_g, w_in, ret_log_decay, rwkv_shift_mu,
           rwkv_w0, rwkv_w_up, rwkv_a0, rwkv_a_up, rwkv_g_up, rwkv_k_k, rwkv_k_a, rwkv_r_k,
           rwkv_ln_w, rwkv_ln_b, w_out, w_ff1, b_ff1, w_ff2, b_ff2, final_g):
    bsz, t, d = x.shape
    assert w_ada.shape[0] == 1, "single trunk layer"
    assert bsz % RWKV_BATCH_BLOCK == 0
    l = 0

    rows = -(-(bsz + 1) // SUBLANES) * SUBLANES
    cc = jnp.zeros((rows, d), F32).at[:bsz].set(c).at[bsz].set(c_ctx)
    mods = _ada(cc, w_ada[l], b_ada[l][None, :])
    sh1, sc1, g1, sh2, sc2, g2 = [m[:bsz, None, :] for m in jnp.split(mods, 6, axis=-1)]
    csh1, csc1 = [jnp.broadcast_to(m[bsz][None, None, :], (bsz, 1, d))
                  for m in jnp.split(mods, 6, axis=-1)[:2]]

    w_in_b = w_in[l].astype(BF16)
    w_ret, w_rw = w_in_b[:, :RET_COLS], w_in_b[:, RET_COLS:]
    n1g = norm1_g[l][None, :]
    mu = rwkv_shift_mu[l]
    p_ret, p_rw = _inproj(x, sh1, sc1, n1g, w_ret, w_rw, mu, tm=TOKEN_TILE)
    pc_ret, pc_rw = _inproj(ctx, csh1, csc1, n1g, w_ret, w_rw, mu, tm=ctx.shape[1])

    cos_t, sin_t = _rope_tables(t)
    ld = jnp.broadcast_to(ret_log_decay[l][:, :, None, None], (2, RET_HEADS, SUBLANES, LANES))
    ret_out = _retention(p_ret, pc_ret, cos_t, sin_t, ld)

    head_id = jnp.arange(RWKV_WIDTH) // RWKV_HEAD_DIM
    red = (head_id[:, None] == jnp.arange(LANES)[None, :]).astype(BF16)
    bc2 = jnp.concatenate([red.T, red.T], axis=0)
    rw_params = (rwkv_w0[l], rwkv_w_up[l].astype(BF16), rwkv_a0[l],
                 rwkv_a_up[l].astype(BF16), rwkv_g_up[l].astype(BF16),
                 rwkv_k_k[l][None, :], rwkv_k_a[l][None, :], rwkv_r_k[l][None, :], red, bc2)
    zeros = jnp.zeros((bsz, 2, RWKV_STATE_PAIRS, LANES, LANES), F32)
    s_ctx = _rwkv_scan(pc_rw, zeros, *rw_params)[4]
    yf, yb, bonus, gate, _ = _rwkv_scan(p_rw, s_ctx, *rw_params)

    wo = w_out[l].astype(BF16)
    return _mlp(x, ret_out, yf, yb, bonus, gate, rwkv_ln_w[l][None, :], rwkv_ln_b[l][None, :], red, bc2,
                g1, sh2, sc2, g2, norm2_g[l][None, :], final_g[None, :],
                wo[:RET_WIDTH], wo[RET_WIDTH:], w_ff1[l].astype(BF16), b_ff1[l][None, :],
                w_ff2[l].astype(BF16), b_ff2[l][None, :], tm=TOKEN_TILE)
```

```python
import functools
import math

import jax
import jax.numpy as jnp
from jax import lax
from jax.experimental import pallas as pl
from jax.experimental.pallas import tpu as pltpu

F32 = jnp.float32
BF16 = jnp.bfloat16

GRID_W = 64
RET_HEADS = 4
RET_HEAD_DIM = 128
RET_WIDTH = RET_HEADS * RET_HEAD_DIM
RET_CHUNK = 128
RWKV_HEAD_DIM = 64
RWKV_WIDTH = 512
RWKV_GROUP_HEADS = 2
RWKV_GROUP_LANES = RWKV_GROUP_HEADS * RWKV_HEAD_DIM
RWKV_GROUPS = RWKV_WIDTH // RWKV_GROUP_LANES
RWKV_STATE_PAIRS = RWKV_WIDTH // 128
RWKV_CHUNK = 64
RWKV_BATCH_BLOCK = 8
RWKV_CHAIN_ROWS = 2
PREP_EVERY_CHAIN_STAGES = 3
INV_BASE_BLOCK = 16
DECAY_LORA = 64
AAA_LORA = 64
GATE_LORA = 128
SHIFT_COLS = 3 * RWKV_WIDTH + DECAY_LORA + AAA_LORA + GATE_LORA
RET_COLS = 4 * RET_WIDTH
ROPE_BASE = 10000.0
NORM_EPS = 1e-6
GN_EPS = 64e-5
W_DECAY_SCALE = math.exp(-0.5)
LANES = 128
SUBLANES = 8
TOKEN_TILE = 512
MERGE_ROW_BLOCK = 128
VMEM_LIMIT = 56 * 1024 * 1024


def _params(*sem):
    return pltpu.CompilerParams(dimension_semantics=sem, vmem_limit_bytes=VMEM_LIMIT)


def _mm(a, b):
    return jnp.dot(a.astype(BF16), b.astype(BF16), preferred_element_type=F32)


def _mm_nt(a, b):
    return lax.dot_general(a.astype(BF16), b.astype(BF16), (((1,), (1,)), ((), ())),
                           preferred_element_type=F32)


def _mm_tn(a, b):
    return lax.dot_general(a.astype(BF16), b.astype(BF16), (((0,), (0,)), ((), ())),
                           preferred_element_type=F32)


def _split3(x):
    hi = x.astype(BF16)
    r1 = x - hi.astype(F32)
    mid = r1.astype(BF16)
    lo = (r1 - mid.astype(F32)).astype(BF16)
    return hi, mid, lo


def _mm_exact_rhs(a_bf16, x):
    return jnp.dot(jnp.concatenate([a_bf16] * 3, axis=1), jnp.concatenate(_split3(x), axis=0),
                   preferred_element_type=F32)


def _split2(x):
    hi = x.astype(BF16)
    return hi, (x - hi.astype(F32)).astype(BF16)


def _rms(x, g):
    return x * lax.rsqrt(jnp.mean(x * x, axis=-1, keepdims=True) + NORM_EPS) * g


def _ada_kernel(c_ref, w_ref, b_ref, o_ref):
    s = jax.nn.silu(c_ref[...])
    o_ref[...] = jnp.dot(s, w_ref[...], preferred_element_type=F32,
                         precision=lax.Precision.HIGHEST) + b_ref[...]


def _ada(cc, w, b):
    rows, d = cc.shape
    n = w.shape[1]
    tn = n // 4
    return pl.pallas_call(
        _ada_kernel,
        grid=(n // tn,),
        in_specs=[pl.BlockSpec((rows, d), lambda j: (0, 0)),
                  pl.BlockSpec((d, tn), lambda j: (0, j)),
                  pl.BlockSpec((1, tn), lambda j: (0, j))],
        out_specs=pl.BlockSpec((rows, tn), lambda j: (0, j)),
        out_shape=jax.ShapeDtypeStruct((rows, n), F32),
        compiler_params=_params("arbitrary"),
        name="adaln",
    )(cc, w, b)


def _inproj_kernel(x_ref, xp_ref, xn_ref, sh_ref, sc_ref, g_ref, wr_ref, ww_ref, mu_ref,
                   pret_ref, prw_ref, ext_s):
    i = pl.program_id(1)
    n = pl.num_programs(1)
    tm = x_ref.shape[1]
    halo = SUBLANES

    def modulated(xv):
        return _rms(xv, g_ref[...]) * (1.0 + sc_ref[0]) + sh_ref[0]

    h = modulated(x_ref[0])
    h_ext = jnp.concatenate([modulated(xp_ref[0]), h, modulated(xn_ref[0])], axis=0).astype(BF16)
    p_ext = jnp.dot(h_ext, ww_ref[...], preferred_element_type=F32)
    rid = lax.broadcasted_iota(jnp.int32, (tm + 2 * halo, 1), 0)
    inside = jnp.logical_and(jnp.logical_or(rid >= halo, i > 0),
                             jnp.logical_or(rid < tm + halo, i < n - 1))
    ext_s[...] = jnp.where(inside, p_ext, 0.0)
    pret_ref[0] = jnp.dot(h.astype(BF16), wr_ref[...], preferred_element_type=F32)
    p = ext_s[halo:halo + tm, :]
    prev = ext_s[halo - 1:halo - 1 + tm, :]
    nxt = ext_s[halo + 1:halo + 1 + tm, :]
    prw_ref[0] = p + mu_ref[0:1, :] * (prev - p) + mu_ref[1:2, :] * (nxt - p)


def _inproj(x, shift, scale, g, w_ret, w_rw, mu, tm):
    bsz, t, d = x.shape
    const = lambda b, i: (0, 0)
    blk8 = tm // SUBLANES
    last8 = t // SUBLANES - 1
    return pl.pallas_call(
        _inproj_kernel,
        grid=(bsz, t // tm),
        in_specs=[pl.BlockSpec((1, tm, d), lambda b, i: (b, i, 0)),
                  pl.BlockSpec((1, SUBLANES, d), lambda b, i: (b, jnp.maximum(i * blk8 - 1, 0), 0)),
                  pl.BlockSpec((1, SUBLANES, d), lambda b, i: (b, jnp.minimum((i + 1) * blk8, last8), 0)),
                  pl.BlockSpec((1, 1, d), lambda b, i: (b, 0, 0)),
                  pl.BlockSpec((1, 1, d), lambda b, i: (b, 0, 0)),
                  pl.BlockSpec((1, d), const),
                  pl.BlockSpec(w_ret.shape, const, pipeline_mode=pl.Buffered(1)),
                  pl.BlockSpec(w_rw.shape, const, pipeline_mode=pl.Buffered(1)),
                  pl.BlockSpec(mu.shape, const)],
        out_specs=[pl.BlockSpec((1, tm, RET_COLS), lambda b, i: (b, i, 0)),
                   pl.BlockSpec((1, tm, SHIFT_COLS), lambda b, i: (b, i, 0))],
        out_shape=[jax.ShapeDtypeStruct((bsz, t, RET_COLS), F32),
                   jax.ShapeDtypeStruct((bsz, t, SHIFT_COLS), F32)],
        scratch_shapes=[pltpu.VMEM((tm + 2 * SUBLANES, SHIFT_COLS), F32)],
        compiler_params=_params("parallel", "parallel"),
        name="inproj",
    )(x, x, x, shift, scale, g, w_ret, w_rw, mu)


def _ret_kernel(q_ref, k_ref, v_ref, g_ref, kc_ref, vc_ref, cos_ref, sin_ref, ld_ref, o_ref):
    c = RET_CHUNK
    hd = RET_HEAD_DIM
    t = q_ref.shape[1]
    tc = kc_ref.shape[1]
    n = t // c
    k_scale = hd ** -0.5
    lgf = -jnp.exp(ld_ref[0, 0][0:1, :])
    lgb = -jnp.exp(ld_ref[1, 0][0:1, :])

    jc = lax.broadcasted_iota(jnp.int32, (tc, 1), 0).astype(F32)
    kc = kc_ref[0] * k_scale
    vc = vc_ref[0]
    s_f = _mm_tn(kc * jnp.exp(lgf * (tc - 1.0 - jc)), vc)
    s_b = _mm_tn(kc * jnp.exp(lgb * jc), vc)

    lane = lax.broadcasted_iota(jnp.int32, (c, hd), 1)
    first_half = (lane % 64) < 32

    def rope(x, cs, sn):
        swapped = jnp.where(first_half, pltpu.roll(x, 96, axis=1), pltpu.roll(x, 32, axis=1))
        return x * cs + swapped * sn

    ii = lax.broadcasted_iota(jnp.int32, (c, 1), 0).astype(F32)
    dist = (lax.broadcasted_iota(jnp.int32, (c, c), 0)
            - lax.broadcasted_iota(jnp.int32, (c, c), 1)).astype(F32)
    decay = jnp.where(dist >= 0, jnp.exp(lgf * jnp.maximum(dist, 0.0)),
                      jnp.exp(lgb * jnp.maximum(-dist, 0.0)))
    qdec_f = jnp.exp(lgf * (ii + 1.0))
    kdec_f = jnp.exp(lgf * (c - 1.0 - ii))
    qdec_b = jnp.exp(lgb * (c - ii))
    kdec_b = jnp.exp(lgb * ii)
    cdec_f = jnp.exp(lgf * c)
    cdec_b = jnp.exp(lgb * c)

    chunks = [slice(ci * c, (ci + 1) * c) for ci in range(n)]
    qs = [rope(q_ref[0, sl, :], cos_ref[sl, :], sin_ref[sl, :]) for sl in chunks]
    ks = [rope(k_ref[0, sl, :] * k_scale, cos_ref[sl, :], sin_ref[sl, :]) for sl in chunks]
    scores = [_mm_nt(qc, kx) for qc, kx in zip(qs, ks)]
    kvs = [_mm_tn(jnp.concatenate([kx * kdec_f, kx * kdec_b], axis=1), v_ref[0, sl, :])
           for kx, sl in zip(ks, chunks)]

    sfs, sbs = [None] * n, [None] * n
    for ci in range(n):
        sfs[ci] = s_f
        s_f = s_f * cdec_f + kvs[ci][:hd]
    for ci in range(n - 1, -1, -1):
        sbs[ci] = s_b
        s_b = s_b * cdec_b + kvs[ci][hd:]

    outs = [_mm(jnp.concatenate([sc * decay, qc * qdec_f, qc * qdec_b], axis=1),
                jnp.concatenate([v_ref[0, sl, :].astype(BF16), sf.astype(BF16), sb.astype(BF16)], axis=0))
            for sc, qc, sl, sf, sb in zip(scores, qs, chunks, sfs, sbs)]
    for o, sl in zip(outs, chunks):
        o = o * lax.rsqrt(jnp.mean(o * o, axis=-1, keepdims=True) + NORM_EPS)
        o_ref[0, sl, :] = (o * jax.nn.silu(g_ref[0, sl, :])).astype(o_ref.dtype)


def _retention(p_ret, pc_ret, cos_t, sin_t, ld):
    bsz, t, _ = p_ret.shape
    tc = pc_ret.shape[1]
    hd = RET_HEAD_DIM
    col = lambda j: pl.BlockSpec((1, t, hd), lambda b, h: (b, 0, h + RET_HEADS * j))
    colc = lambda j: pl.BlockSpec((1, tc, hd), lambda b, h: (b, 0, h + RET_HEADS * j))
    return pl.pallas_call(
        _ret_kernel,
        grid=(bsz, RET_HEADS),
        in_specs=[col(0), col(1), col(2), col(3), colc(1), colc(2),
                  pl.BlockSpec((t, hd), lambda b, h: (0, 0)),
                  pl.BlockSpec((t, hd), lambda b, h: (0, 0)),
                  pl.BlockSpec((2, 1, SUBLANES, LANES), lambda b, h: (0, h, 0, 0))],
        out_specs=pl.BlockSpec((1, t, hd), lambda b, h: (b, 0, h)),
        out_shape=jax.ShapeDtypeStruct((bsz, t, RET_WIDTH), BF16),
        compiler_params=_params("parallel", "parallel"),
        name="retention",
    )(p_ret, p_ret, p_ret, p_ret, pc_ret, pc_ret, cos_t, sin_t, ld)


def _group_expand(y, head_of_lane):
    zero = jnp.zeros_like(y)
    return jnp.concatenate([jnp.where(head_of_lane == h, y, zero) for h in range(RWKV_GROUP_HEADS)],
                           axis=0)


def _mm_group(a, y, head_of_lane):
    return _mm(a, _group_expand(y.astype(BF16), head_of_lane))


def _rwkv_scan_kernel(pf_ref, pb_ref, w0_ref, wup_ref, a0_ref, aup_ref, gup_ref, kk_ref, ka_ref, rk_ref,
                      red_ref, bc_ref, s0_ref, yf_ref, yb_ref, bonus_ref, gate_ref, fin_ref, st_s):
    c = RWKV_CHUNK
    w_ = RWKV_WIDTH
    hd = RWKV_HEAD_DIM
    nb = pf_ref.shape[0]
    j = pl.program_id(1)
    n = pl.num_programs(1)

    per_group = RWKV_GROUP_LANES // LANES

    def pair_window(ref, p):
        g, o = divmod(p, per_group)
        return ref.at[:, :, g, o * LANES:(o + 1) * LANES, o * LANES:(o + 1) * LANES]

    @pl.when(j == 0)
    def _():
        st_s[...] = jnp.zeros(st_s.shape, F32)
        for p in range(s0_ref.shape[2]):
            pair_window(st_s, p)[...] = s0_ref[:, :, p]

    row = lax.broadcasted_iota(jnp.int32, (c, c), 0)
    colm = lax.broadcasted_iota(jnp.int32, (c, c), 1)
    gl = RWKV_GROUP_LANES
    lane = lax.broadcasted_iota(jnp.int32, (c, gl), 1)
    trow = lax.broadcasted_iota(jnp.int32, (c, gl), 0)
    head_of = lane // hd
    jj = lane % hd
    r2 = lax.broadcasted_iota(jnp.int32, (gl, gl), 0)
    c2 = lax.broadcasted_iota(jnp.int32, (gl, gl), 1)
    blockdiag = (r2 // hd) == (c2 // hd)
    eye_cat = jnp.where(jj == trow, 1.0, 0.0)
    red = red_ref[...]
    bc2 = bc_ref[...]

    def same_block(size):
        return (trow // size) == (jj // size)

    def stacked(fn, xs):
        out = fn(jnp.concatenate(xs, axis=0))
        return [out[i * c:(i + 1) * c] for i in range(len(xs))]

    def head_sums(xs):
        return stacked(lambda x: jnp.dot(x.astype(BF16), red, preferred_element_type=F32), xs)

    def head_bcast(ss):
        return stacked(lambda s: jnp.dot(jnp.concatenate(_split2(s), axis=1), bc2,
                                         preferred_element_type=F32), ss)

    def prep_stages(rows, out):
        items = [dict(bi=bi, d=d, fwd=d == 0) for bi in rows for d in range(2)]
        for it in items:
            ps = (pf_ref if it["fwd"] else pb_ref)[it["bi"]]
            k = ps[:, w_:2 * w_]
            it.update(ps=ps, k=k, kk0=k * kk_ref[...])
        for it, s in zip(items, head_sums([it["kk0"] * it["kk0"] for it in items])):
            it["nrm"] = s
        for d in range(2):
            sel = [it for it in items if it["d"] == d]
            wls = [jnp.tanh(it["ps"][:, 3 * w_:3 * w_ + DECAY_LORA]) for it in sel]
            als = [it["ps"][:, 3 * w_ + DECAY_LORA:3 * w_ + DECAY_LORA + AAA_LORA] for it in sel]
            for it, wp, ap in zip(sel, stacked(lambda x: _mm(x, wup_ref[d]), wls),
                                  stacked(lambda x: _mm(x, aup_ref[d]), als)):
                it.update(w_pre=wp, a_pre=ap)
        yield
        for it, nb_ in zip(items, head_bcast([lax.rsqrt(it["nrm"] + 1e-12) for it in items])):
            it["inv_norm"] = nb_
        for it in items:
            d = it["d"]
            logw = -W_DECAY_SCALE * jax.nn.sigmoid(w0_ref[d:d + 1, :] + it["w_pre"])
            tri = ((colm <= row) if it["fwd"] else (colm >= row)).astype(BF16)
            it.update(logw=logw, big_l=_mm_exact_rhs(tri, logw),
                      a=jax.nn.sigmoid(a0_ref[d:d + 1, :] + it["a_pre"]))
        yield
        for it in items:
            bi, fwd, ps, k, a, logw, big_l = (it[x] for x in ("bi", "fwd", "ps", "k", "a", "logw", "big_l"))
            r = ps[:, 0:w_]
            v = ps[:, 2 * w_:3 * w_]
            kk = it["kk0"] * it["inv_norm"]
            kt = k * (1.0 + (a - 1.0) * ka_ref[...])
            b = kk * a
            it.update(v=v, rkt=r * kt * rk_ref[...])
            ltot = big_l[c - 1:c, :] if fwd else big_l[0:1, :]
            e_lx = jnp.exp(big_l - logw)
            e_nl = jnp.exp(-big_l)
            e_tot = jnp.exp(ltot)
            e_rest = e_tot * e_nl
            strict = (jj < trow) if fwd else (jj > trow)
            out.append(dict(bi=bi, d=it["d"], v=v, at=-kk * e_lx, bt=b * e_nl, ktl=kt * e_nl,
                            rt=r * (jnp.exp(big_l) if fwd else e_lx), bh=b * e_rest, kh=kt * e_rest,
                            e_tot=e_tot, strict=strict,
                            omask=(jj <= trow) if fwd else strict))
        fw = [it for it in items if it["fwd"]]
        gls = [jax.nn.sigmoid(it["ps"][:, 3 * w_ + DECAY_LORA + AAA_LORA:]) for it in fw]
        for it, g_ in zip(fw, stacked(lambda x: _mm(x, gup_ref[...]), gls)):
            gate_ref[it["bi"]] = g_
        bsums = head_sums([it["rkt"] for it in fw])
        yield
        for it, bb in zip(fw, head_bcast(bsums)):
            bonus_ref[it["bi"]] = bb * it["v"]

    def chain_stages(ops):
        chains = []
        for po in ops:
            bi, d = po["bi"], po["d"]
            for pr in range(RWKV_GROUPS):
                sl = slice(pr * gl, (pr + 1) * gl)
                lhs = jnp.concatenate([po["at"][:, sl], po["rt"][:, sl]], axis=0)
                m = _mm_nt(lhs, jnp.concatenate([_group_expand(po["bt"][:, sl].astype(BF16), head_of),
                                                 _group_expand(po["ktl"][:, sl].astype(BF16), head_of)], axis=0))
                s_in = st_s[bi, d, pr]
                ch = dict(bi=bi, d=d, pr=pr, sl=sl, v=po["v"][:, sl], bh=po["bh"][:, sl],
                          kh=po["kh"][:, sl], e_tot=po["e_tot"][:, sl], s_in=s_in, sx=_mm_nt(lhs, s_in),
                          a_ab=jnp.where(po["strict"], m[:c, :gl], 0.0),
                          a_ak=jnp.where(po["strict"], m[:c, gl:], 0.0),
                          m_rb=jnp.where(po["omask"], m[c:, :gl], 0.0),
                          m_rk=jnp.where(po["omask"], m[c:, gl:], 0.0))
                chains.append(ch)
        yield

        a_diag = [jnp.where(same_block(INV_BASE_BLOCK), ch["a_ab"], 0.0) for ch in chains]
        tinv = [eye_cat + a for a in a_diag]
        pw = [_mm_group(p, p, head_of) for p in a_diag]
        yield
        span = 4
        while span < INV_BASE_BLOCK:
            both = [_mm_group(jnp.concatenate([t, p], axis=0), p, head_of) for t, p in zip(tinv, pw)]
            tinv = [t + r[:c] for t, r in zip(tinv, both)]
            pw = [r[c:] for r in both]
            yield
            span *= 2
        tinv = [t + _mm_group(t, p, head_of) for t, p in zip(tinv, pw)]
        yield
        size = INV_BASE_BLOCK
        while size < c:
            off = jnp.logical_and(same_block(2 * size), jnp.logical_not(same_block(size)))
            ta = [_mm_group(t, jnp.where(off, ch["a_ab"], 0.0), head_of) for t, ch in zip(tinv, chains)]
            yield
            tinv = [t + _mm_group(x, t, head_of) for t, x in zip(tinv, ta)]
            yield
            size *= 2

        avs = [_mm(jnp.concatenate([ch["a_ak"], ch["m_rk"]], axis=0),
                   _group_expand(ch["v"].astype(BF16), head_of)) for ch in chains]
        yield
        us = [_mm_group(t_m, ch["sx"][:c] + av[:c], head_of) for ch, t_m, av in zip(chains, tinv, avs)]
        yield
        for ch, av, u in zip(chains, avs, us):
            y_ref = yf_ref if ch["d"] == 0 else yb_ref
            y_ref[ch["bi"], :, ch["sl"]] = ch["sx"][c:] + av[c:] + _mm_group(ch["m_rb"], u, head_of)
        upds = [_mm_tn(jnp.concatenate([u, ch["v"]], axis=0), jnp.concatenate([ch["bh"], ch["kh"]], axis=0))
                for ch, u in zip(chains, us)]
        yield
        for ch, upd in zip(chains, upds):
            st_s[ch["bi"], ch["d"], ch["pr"]] = ch["s_in"] * ch["e_tot"] + jnp.where(blockdiag, upd, 0.0)

    groups = [range(g, min(g + RWKV_CHAIN_ROWS, nb)) for g in range(0, nb, RWKV_CHAIN_ROWS)]
    ops = []
    for _ in prep_stages(groups[0], ops):
        pass
    for gi in range(len(groups)):
        nxt = []
        prep = prep_stages(groups[gi + 1], nxt) if gi + 1 < len(groups) else iter(())
        for lvl, _ in enumerate(chain_stages(ops)):
            if lvl % PREP_EVERY_CHAIN_STAGES == 0:
                next(prep, None)
        for _ in prep:
            pass
        ops = nxt

    @pl.when(j == n - 1)
    def _():
        for p in range(fin_ref.shape[2]):
            fin_ref[:, :, p] = pair_window(st_s, p)[...]


def _rwkv_scan(p_rw, s0, w0, wup, a0, aup, gup, k_k, k_a, r_k, red, bc2):
    bsz, t, cols = p_rw.shape
    c = RWKV_CHUNK
    nb = RWKV_BATCH_BLOCK
    n = t // c
    const2 = lambda b, i: (0, 0)
    const3 = lambda b, i: (0, 0, 0)
    state_blk = (nb, 2, RWKV_STATE_PAIRS, LANES, LANES)
    state_map = lambda b, i: (b, 0, 0, 0, 0)
    state_shape = jax.ShapeDtypeStruct((bsz, 2, RWKV_STATE_PAIRS, LANES, LANES), F32)
    tok_shape = jax.ShapeDtypeStruct((bsz, t, RWKV_WIDTH), F32)
    tok_f = pl.BlockSpec((nb, c, RWKV_WIDTH), lambda b, i: (b, i, 0))
    tok_b = pl.BlockSpec((nb, c, RWKV_WIDTH), lambda b, i: (b, n - 1 - i, 0))
    return pl.pallas_call(
        _rwkv_scan_kernel,
        grid=(bsz // nb, n),
        in_specs=[
            pl.BlockSpec((nb, c, cols), lambda b, i: (b, i, 0)),
            pl.BlockSpec((nb, c, cols), lambda b, i: (b, n - 1 - i, 0)),
            pl.BlockSpec(w0.shape, const2),
            pl.BlockSpec(wup.shape, const3), pl.BlockSpec(a0.shape, const2),
            pl.BlockSpec(aup.shape, const3), pl.BlockSpec(gup.shape, const2),
            pl.BlockSpec(k_k.shape, const2), pl.BlockSpec(k_a.shape, const2),
            pl.BlockSpec(r_k.shape, const2), pl.BlockSpec(red.shape, const2),
            pl.BlockSpec(bc2.shape, const2),
            pl.BlockSpec(state_blk, state_map, pipeline_mode=pl.Buffered(1))],
        out_specs=[tok_f, tok_b, tok_f, tok_f, pl.BlockSpec(state_blk, state_map)],
        out_shape=[tok_shape] * 4 + [state_shape],
        scratch_shapes=[pltpu.VMEM((nb, 2, RWKV_GROUPS, RWKV_GROUP_LANES, RWKV_GROUP_LANES), F32)],
        compiler_params=_params("parallel", "arbitrary"),
        name="rwkv_scan",
    )(p_rw, p_rw, w0, wup, a0, aup, gup, k_k, k_a, r_k, red, bc2, s0)


def _mlp_kernel(x_ref, ret_ref, yf_ref, yb_ref, bonus_ref, gate_ref, lnw_ref, lnb_ref, red_ref, bc_ref,
                g1_ref, sh2_ref, sc2_ref, g2_ref, n2g_ref, fg_ref,
                woa_ref, wob_ref, w1_ref, b1_ref, w2_ref, b2_ref, o_ref, *, ff_tile):
    mix_ret = jnp.dot(ret_ref[0], woa_ref[...], preferred_element_type=F32)
    red = red_ref[...]
    bc2 = bc_ref[...]
    inv_n = 1.0 / RWKV_HEAD_DIM
    blk = MERGE_ROW_BLOCK
    rows = [slice(r, r + blk) for r in range(0, x_ref.shape[1], blk)]

    def head_means(xs):
        ss = [jnp.dot(jnp.concatenate(_split2(x), axis=0), red, preferred_element_type=F32) for x in xs]
        return [jnp.dot(jnp.concatenate(_split2(s[:blk] + s[blk:]), axis=1), bc2,
                        preferred_element_type=F32) * inv_n for s in ss]

    ys = [yf_ref[0, r, :] + yb_ref[0, r, :] for r in rows]
    ycs = [y - m for y, m in zip(ys, head_means(ys))]
    variances = head_means([yc * yc for yc in ycs])
    rw = jnp.concatenate(
        [((yc * lax.rsqrt(var + GN_EPS) * lnw_ref[...] + lnb_ref[...] + bonus_ref[0, r, :])
          * gate_ref[0, r, :]).astype(BF16) for yc, var, r in zip(ycs, variances, rows)], axis=0)
    mix = mix_ret + jnp.dot(rw, wob_ref[...], preferred_element_type=F32)
    h1 = x_ref[0] + g1_ref[0] * mix
    n2 = (_rms(h1, n2g_ref[...]) * (1.0 + sc2_ref[0]) + sh2_ref[0]).astype(BF16)
    d_ff = w1_ref.shape[1]
    slices = [slice(s * ff_tile, (s + 1) * ff_tile) for s in range(d_ff // ff_tile)]

    def up(sl):
        return jnp.dot(n2, w1_ref[:, sl], preferred_element_type=F32) + b1_ref[:, sl]

    acc = jnp.zeros(h1.shape, F32)
    hid = up(slices[0])
    for s, sl in enumerate(slices):
        nxt = up(slices[s + 1]) if s + 1 < len(slices) else None
        act = jnp.square(jnp.maximum(hid, 0.0)).astype(BF16)
        acc = acc + jnp.dot(act, w2_ref[sl, :], preferred_element_type=F32)
        hid = nxt
    h2 = h1 + g2_ref[0] * (acc + b2_ref[...])
    o_ref[0] = _rms(h2, fg_ref[...])


def _mlp(x, ret, yf, yb, bonus, gate, ln_w, ln_b, red, bc2, g1, sh2, sc2, g2, n2g, fg,
         woa, wob, w1, b1, w2, b2, tm):
    bsz, t, d = x.shape
    const = lambda b, i: (0, 0)
    single = pl.Buffered(1)
    mod = pl.BlockSpec((1, 1, d), lambda b, i: (b, 0, 0))
    cw = lambda a: pl.BlockSpec(a.shape, const, pipeline_mode=single)
    tok = lambda w: pl.BlockSpec((1, tm, w), lambda b, i: (b, i, 0))
    return pl.pallas_call(
        functools.partial(_mlp_kernel, ff_tile=1024),
        grid=(bsz, t // tm),
        in_specs=[tok(d), tok(RET_WIDTH), tok(RWKV_WIDTH), tok(RWKV_WIDTH), tok(RWKV_WIDTH),
                  tok(RWKV_WIDTH), cw(ln_w), cw(ln_b), cw(red), cw(bc2),
                  mod, mod, mod, mod, cw(n2g), cw(fg),
                  cw(woa), cw(wob), cw(w1), cw(b1), cw(w2), cw(b2)],
        out_specs=tok(d),
        out_shape=jax.ShapeDtypeStruct((bsz, t, d), F32),
        compiler_params=_params("parallel", "parallel"),
        name="outproj_mlp",
    )(x, ret, yf, yb, bonus, gate, ln_w, ln_b, red, bc2, g1, sh2, sc2, g2, n2g, fg,
      woa, wob, w1, b1, w2, b2)


def _rope_tables(t):
    half = RET_HEAD_DIM // 2
    inv = jnp.power(ROPE_BASE, -jnp.arange(0, half, 2, dtype=F32) / half)
    pos = jnp.arange(t)
    ang_r = (pos // GRID_W).astype(F32)[:, None] * inv[None, :]
    ang_c = (pos % GRID_W).astype(F32)[:, None] * inv[None, :]
    cos_t = jnp.concatenate([jnp.cos(ang_r)] * 2 + [jnp.cos(ang_c)] * 2, axis=-1)
    sin_t = jnp.concatenate([-jnp.sin(ang_r), jnp.sin(ang_r), -jnp.sin(ang_c), jnp.sin(ang_c)], axis=-1)
    return cos_t, sin_t


def kernel(x, c, ctx, c_ctx, w_ada, b_ada, norm1_g, norm2_g, w_in, ret_log_decay, rwkv_shift_mu,
           rwkv_w0, rwkv_w_up, rwkv_a0, rwkv_a_up, rwkv_g_up, rwkv_k_k, rwkv_k_a, rwkv_r_k,
           rwkv_ln_w, rwkv_ln_b, w_out, w_ff1, b_ff1, w_ff2, b_ff2, final_g):
    bsz, t, d = x.shape
    assert w_ada.shape[0] == 1, "single trunk layer"
    assert bsz % RWKV_BATCH_BLOCK == 0
    l = 0

    rows = -(-(bsz + 1) // SUBLANES) * SUBLANES
    cc = jnp.zeros((rows, d), F32).at[:bsz].set(c).at[bsz].set(c_ctx)
    mods = _ada(cc, w_ada[l], b_ada[l][None, :])
    sh1, sc1, g1, sh2, sc2, g2 = [m[:bsz, None, :] for m in jnp.split(mods, 6, axis=-1)]
    csh1, csc1 = [jnp.broadcast_to(m[bsz][None, None, :], (bsz, 1, d))
                  for m in jnp.split(mods, 6, axis=-1)[:2]]

    w_in_b = w_in[l].astype(BF16)
    w_ret, w_rw = w_in_b[:, :RET_COLS], w_in_b[:, RET_COLS:]
    n1g = norm1_g[l][None, :]
    mu = rwkv_shift_mu[l]
    p_ret, p_rw = _inproj(x, sh1, sc1, n1g, w_ret, w_rw, mu, tm=TOKEN_TILE)
    pc_ret, pc_rw = _inproj(ctx, csh1, csc1, n1g, w_ret, w_rw, mu, tm=ctx.shape[1])

    cos_t, sin_t = _rope_tables(t)
    ld = jnp.broadcast_to(ret_log_decay[l][:, :, None, None], (2, RET_HEADS, SUBLANES, LANES))
    ret_out = _retention(p_ret, pc_ret, cos_t, sin_t, ld)

    head_id = jnp.arange(RWKV_WIDTH) // RWKV_HEAD_DIM
    red = (head_id[:, None] == jnp.arange(LANES)[None, :]).astype(BF16)
    bc2 = jnp.concatenate([red.T, red.T], axis=0)
    rw_params = (rwkv_w0[l], rwkv_w_up[l].astype(BF16), rwkv_a0[l],
                 rwkv_a_up[l].astype(BF16), rwkv_g_up[l].astype(BF16),
                 rwkv_k_k[l][None, :], rwkv_k_a[l][None, :], rwkv_r_k[l][None, :], red, bc2)
    zeros = jnp.zeros((bsz, 2, RWKV_STATE_PAIRS, LANES, LANES), F32)
    s_ctx = _rwkv_scan(pc_rw, zeros, *rw_params)[4]
    yf, yb, bonus, gate, _ = _rwkv_scan(p_rw, s_ctx, *rw_params)

    wo = w_out[l].astype(BF16)
    return _mlp(x, ret_out, yf, yb, bonus, gate, rwkv_ln_w[l][None, :], rwkv_ln_b[l][None, :], red, bc2,
                g1, sh2, sc2, g2, norm2_g[l][None, :], final_g[None, :],
                wo[:RET_WIDTH], wo[RET_WIDTH:], w_ff1[l].astype(BF16), b_ff1[l][None, :],
                w_ff2[l].astype(BF16), b_ff2[l][None, :], tm=TOKEN_TILE)
```

```python
import functools
import math

import jax
import jax.numpy as jnp
from jax import lax
from jax.experimental import pallas as pl
from jax.experimental.pallas import tpu as pltpu

F32 = jnp.float32
BF16 = jnp.bfloat16

GRID_W = 64
RET_HEADS = 4
RET_HEAD_DIM = 128
RET_WIDTH = RET_HEADS * RET_HEAD_DIM
RET_CHUNK = 128
RWKV_HEAD_DIM = 64
RWKV_WIDTH = 512
RWKV_GROUP_HEADS = 2
RWKV_GROUP_LANES = RWKV_GROUP_HEADS * RWKV_HEAD_DIM
RWKV_GROUPS = RWKV_WIDTH // RWKV_GROUP_LANES
RWKV_STATE_PAIRS = RWKV_WIDTH // 128
RWKV_CHUNK = 64
RWKV_BATCH_BLOCK = 8
RWKV_CHAIN_ROWS = 2
PREP_EVERY_CHAIN_STAGES = 3
INV_BASE_BLOCK = 16
DECAY_LORA = 64
AAA_LORA = 64
GATE_LORA = 128
SHIFT_COLS = 3 * RWKV_WIDTH + DECAY_LORA + AAA_LORA + GATE_LORA
RET_COLS = 4 * RET_WIDTH
ROPE_BASE = 10000.0
NORM_EPS = 1e-6
GN_EPS = 64e-5
W_DECAY_SCALE = math.exp(-0.5)
LANES = 128
SUBLANES = 8
TOKEN_TILE = 512
MERGE_ROW_BLOCK = 128
VMEM_LIMIT = 56 * 1024 * 1024


def _params(*sem):
    return pltpu.CompilerParams(dimension_semantics=sem, vmem_limit_bytes=VMEM_LIMIT)


def _mm(a, b):
    return jnp.dot(a.astype(BF16), b.astype(BF16), preferred_element_type=F32)


def _mm_nt(a, b):
    return lax.dot_general(a.astype(BF16), b.astype(BF16), (((1,), (1,)), ((), ())),
                           preferred_element_type=F32)


def _mm_tn(a, b):
    return lax.dot_general(a.astype(BF16), b.astype(BF16), (((0,), (0,)), ((), ())),
                           preferred_element_type=F32)


def _split3(x):
    hi = x.astype(BF16)
    r1 = x - hi.astype(F32)
    mid = r1.astype(BF16)
    lo = (r1 - mid.astype(F32)).astype(BF16)
    return hi, mid, lo


def _mm_exact_rhs(a_bf16, x):
    return jnp.dot(jnp.concatenate([a_bf16] * 3, axis=1), jnp.concatenate(_split3(x), axis=0),
                   preferred_element_type=F32)


def _split2(x):
    hi = x.astype(BF16)
    return hi, (x - hi.astype(F32)).astype(BF16)


def _rms(x, g):
    return x * lax.rsqrt(jnp.mean(x * x, axis=-1, keepdims=True) + NORM_EPS) * g


def _ada_kernel(c_ref, w_ref, b_ref, o_ref):
    s = jax.nn.silu(c_ref[...])
    o_ref[...] = jnp.dot(s, w_ref[...], preferred_element_type=F32,
                         precision=lax.Precision.HIGHEST) + b_ref[...]


def _ada(cc, w, b):
    rows, d = cc.shape
    n = w.shape[1]
    tn = n // 4
    return pl.pallas_call(
        _ada_kernel,
        grid=(n // tn,),
        in_specs=[pl.BlockSpec((rows, d), lambda j: (0, 0)),
                  pl.BlockSpec((d, tn), lambda j: (0, j)),
                  pl.BlockSpec((1, tn), lambda j: (0, j))],
        out_specs=pl.BlockSpec((rows, tn), lambda j: (0, j)),
        out_shape=jax.ShapeDtypeStruct((rows, n), F32),
        compiler_params=_params("arbitrary"),
        name="adaln",
    )(cc, w, b)


def _inproj_kernel(x_ref, xp_ref, xn_ref, sh_ref, sc_ref, g_ref, wr_ref, ww_ref, mu_ref,
                   pret_ref, prw_ref):
    i = pl.program_id(1)
    n = pl.num_programs(1)
    tm = x_ref.shape[1]
    halo = SUBLANES

    gain = g_ref[...] * (1.0 + sc_ref[0])

    def modulated(xv):
        return xv * lax.rsqrt(jnp.mean(xv * xv, axis=-1, keepdims=True) + NORM_EPS) * gain + sh_ref[0]

    h = modulated(x_ref[0])
    h_ext = jnp.concatenate([modulated(xp_ref[0]), h, modulated(xn_ref[0])], axis=0).astype(BF16)
    p_ext = jnp.dot(h_ext, ww_ref[...], preferred_element_type=F32)
    rid = lax.broadcasted_iota(jnp.int32, (tm + 2 * halo, 1), 0)
    inside = jnp.logical_and(jnp.logical_or(rid >= halo, i > 0),
                             jnp.logical_or(rid < tm + halo, i < n - 1))
    ext = jnp.where(inside, p_ext, 0.0)
    pret_ref[0] = jnp.dot(h.astype(BF16), wr_ref[...], preferred_element_type=F32)
    rows = tm + 2 * halo
    p = ext[halo:halo + tm]
    prev = pltpu.roll(ext, 1, axis=0)[halo:halo + tm]
    nxt = pltpu.roll(ext, rows - 1, axis=0)[halo:halo + tm]
    mu_prev, mu_next = mu_ref[0:1, :], mu_ref[1:2, :]
    prw_ref[0] = (1.0 - mu_prev - mu_next) * p + mu_prev * prev + mu_next * nxt


def _inproj(x, shift, scale, g, w_ret, w_rw, mu, tm):
    bsz, t, d = x.shape
    const = lambda b, i: (0, 0)
    blk8 = tm // SUBLANES
    last8 = t // SUBLANES - 1
    return pl.pallas_call(
        _inproj_kernel,
        grid=(bsz, t // tm),
        in_specs=[pl.BlockSpec((1, tm, d), lambda b, i: (b, i, 0)),
                  pl.BlockSpec((1, SUBLANES, d), lambda b, i: (b, jnp.maximum(i * blk8 - 1, 0), 0)),
                  pl.BlockSpec((1, SUBLANES, d), lambda b, i: (b, jnp.minimum((i + 1) * blk8, last8), 0)),
                  pl.BlockSpec((1, 1, d), lambda b, i: (b, 0, 0)),
                  pl.BlockSpec((1, 1, d), lambda b, i: (b, 0, 0)),
                  pl.BlockSpec((1, d), const),
                  pl.BlockSpec(w_ret.shape, const, pipeline_mode=pl.Buffered(1)),
                  pl.BlockSpec(w_rw.shape, const, pipeline_mode=pl.Buffered(1)),
                  pl.BlockSpec(mu.shape, const)],
        out_specs=[pl.BlockSpec((1, tm, RET_COLS), lambda b, i: (b, i, 0)),
                   pl.BlockSpec((1, tm, SHIFT_COLS), lambda b, i: (b, i, 0))],
        out_shape=[jax.ShapeDtypeStruct((bsz, t, RET_COLS), F32),
                   jax.ShapeDtypeStruct((bsz, t, SHIFT_COLS), F32)],
        compiler_params=_params("parallel", "parallel"),
        name="inproj",
    )(x, x, x, shift, scale, g, w_ret, w_rw, mu)


def _ret_kernel(q_ref, k_ref, v_ref, g_ref, kc_ref, vc_ref, cos_ref, sin_ref, ld_ref, o_ref):
    c = RET_CHUNK
    hd = RET_HEAD_DIM
    t = q_ref.shape[1]
    tc = kc_ref.shape[1]
    n = t // c
    k_scale = hd ** -0.5
    lgf = -jnp.exp(ld_ref[0, 0][0:1, :])
    lgb = -jnp.exp(ld_ref[1, 0][0:1, :])

    jc = lax.broadcasted_iota(jnp.int32, (tc, 1), 0).astype(F32)
    kc = kc_ref[0] * k_scale
    vc = vc_ref[0]
    s_f = _mm_tn(kc * jnp.exp(lgf * (tc - 1.0 - jc)), vc)
    s_b = _mm_tn(kc * jnp.exp(lgb * jc), vc)

    lane = lax.broadcasted_iota(jnp.int32, (c, hd), 1)
    first_half = (lane % 64) < 32

    def rope(x, cs, sn):
        swapped = jnp.where(first_half, pltpu.roll(x, 96, axis=1), pltpu.roll(x, 32, axis=1))
        return x * cs + swapped * sn

    ii = lax.broadcasted_iota(jnp.int32, (c, 1), 0).astype(F32)
    dist = (lax.broadcasted_iota(jnp.int32, (c, c), 0)
            - lax.broadcasted_iota(jnp.int32, (c, c), 1)).astype(F32)
    decay = jnp.where(dist >= 0, jnp.exp(lgf * jnp.maximum(dist, 0.0)),
                      jnp.exp(lgb * jnp.maximum(-dist, 0.0)))
    qdec_f = jnp.exp(lgf * (ii + 1.0))
    kdec_f = jnp.exp(lgf * (c - 1.0 - ii))
    qdec_b = jnp.exp(lgb * (c - ii))
    kdec_b = jnp.exp(lgb * ii)
    cdec_f = jnp.exp(lgf * c)
    cdec_b = jnp.exp(lgb * c)

    chunks = [slice(ci * c, (ci + 1) * c) for ci in range(n)]
    qs = [rope(q_ref[0, sl, :], cos_ref[sl, :], sin_ref[sl, :]) for sl in chunks]
    ks = [rope(k_ref[0, sl, :] * k_scale, cos_ref[sl, :], sin_ref[sl, :]) for sl in chunks]
    scores = [_mm_nt(qc, kx) for qc, kx in zip(qs, ks)]
    kvs = [_mm_tn(jnp.concatenate([kx * kdec_f, kx * kdec_b], axis=1), v_ref[0, sl, :])
           for kx, sl in zip(ks, chunks)]

    sfs, sbs = [None] * n, [None] * n
    for ci in range(n):
        sfs[ci] = s_f
        s_f = s_f * cdec_f + kvs[ci][:hd]
    for ci in range(n - 1, -1, -1):
        sbs[ci] = s_b
        s_b = s_b * cdec_b + kvs[ci][hd:]

    outs = [_mm(jnp.concatenate([sc * decay, qc * qdec_f, qc * qdec_b], axis=1),
                jnp.concatenate([v_ref[0, sl, :].astype(BF16), sf.astype(BF16), sb.astype(BF16)], axis=0))
            for sc, qc, sl, sf, sb in zip(scores, qs, chunks, sfs, sbs)]
    for o, sl in zip(outs, chunks):
        o = o * lax.rsqrt(jnp.mean(o * o, axis=-1, keepdims=True) + NORM_EPS)
        o_ref[0, sl, :] = (o * jax.nn.silu(g_ref[0, sl, :])).astype(o_ref.dtype)


def _retention(p_ret, pc_ret, cos_t, sin_t, ld):
    bsz, t, _ = p_ret.shape
    tc = pc_ret.shape[1]
    hd = RET_HEAD_DIM
    col = lambda j: pl.BlockSpec((1, t, hd), lambda b, h: (b, 0, h + RET_HEADS * j))
    colc = lambda j: pl.BlockSpec((1, tc, hd), lambda b, h: (b, 0, h + RET_HEADS * j))
    return pl.pallas_call(
        _ret_kernel,
        grid=(bsz, RET_HEADS),
        in_specs=[col(0), col(1), col(2), col(3), colc(1), colc(2),
                  pl.BlockSpec((t, hd), lambda b, h: (0, 0)),
                  pl.BlockSpec((t, hd), lambda b, h: (0, 0)),
                  pl.BlockSpec((2, 1, SUBLANES, LANES), lambda b, h: (0, h, 0, 0))],
        out_specs=pl.BlockSpec((1, t, hd), lambda b, h: (b, 0, h)),
        out_shape=jax.ShapeDtypeStruct((bsz, t, RET_WIDTH), BF16),
        compiler_params=_params("parallel", "parallel"),
        name="retention",
    )(p_ret, p_ret, p_ret, p_ret, pc_ret, pc_ret, cos_t, sin_t, ld)


def _group_expand(y, head_of_lane):
    zero = jnp.zeros_like(y)
    return jnp.concatenate([jnp.where(head_of_lane == h, y, zero) for h in range(RWKV_GROUP_HEADS)],
                           axis=0)


def _mm_group(a, y, head_of_lane):
    return _mm(a, _group_expand(y.astype(BF16), head_of_lane))


def _rwkv_scan_kernel(pf_ref, pb_ref, w0_ref, wup_ref, a0_ref, aup_ref, gup_ref, kk_ref, ka_ref, rk_ref,
                      red_ref, bc_ref, s0_ref, yf_ref, yb_ref, bonus_ref, gate_ref, fin_ref, st_s):
    c = RWKV_CHUNK
    w_ = RWKV_WIDTH
    hd = RWKV_HEAD_DIM
    nb = pf_ref.shape[0]
    j = pl.program_id(1)
    n = pl.num_programs(1)

    per_group = RWKV_GROUP_LANES // LANES

    def pair_window(ref, p):
        g, o = divmod(p, per_group)
        return ref.at[:, :, g, o * LANES:(o + 1) * LANES, o * LANES:(o + 1) * LANES]

    @pl.when(j == 0)
    def _():
        st_s[...] = jnp.zeros(st_s.shape, F32)
        for p in range(s0_ref.shape[2]):
            pair_window(st_s, p)[...] = s0_ref[:, :, p]

    row = lax.broadcasted_iota(jnp.int32, (c, c), 0)
    colm = lax.broadcasted_iota(jnp.int32, (c, c), 1)
    gl = RWKV_GROUP_LANES
    lane = lax.broadcasted_iota(jnp.int32, (c, gl), 1)
    trow = lax.broadcasted_iota(jnp.int32, (c, gl), 0)
    head_of = lane // hd
    jj = lane % hd
    r2 = lax.broadcasted_iota(jnp.int32, (gl, gl), 0)
    c2 = lax.broadcasted_iota(jnp.int32, (gl, gl), 1)
    blockdiag = (r2 // hd) == (c2 // hd)
    eye_cat = jnp.where(jj == trow, 1.0, 0.0)
    red = red_ref[...]
    bc2 = bc_ref[...]

    def same_block(size):
        return (trow // size) == (jj // size)

    def stacked(fn, xs):
        out = fn(jnp.concatenate(xs, axis=0))
        return [out[i * c:(i + 1) * c] for i in range(len(xs))]

    def head_sums(xs):
        return stacked(lambda x: jnp.dot(x.astype(BF16), red, preferred_element_type=F32), xs)

    def head_bcast(ss):
        return stacked(lambda s: jnp.dot(jnp.concatenate(_split2(s), axis=1), bc2,
                                         preferred_element_type=F32), ss)

    def prep_stages(rows, out):
        items = [dict(bi=bi, d=d, fwd=d == 0) for bi in rows for d in range(2)]
        for it in items:
            ps = (pf_ref if it["fwd"] else pb_ref)[it["bi"]]
            k = ps[:, w_:2 * w_]
            it.update(ps=ps, k=k, kk0=k * kk_ref[...])
        for it, s in zip(items, head_sums([it["kk0"] * it["kk0"] for it in items])):
            it["nrm"] = s
        for d in range(2):
            sel = [it for it in items if it["d"] == d]
            wls = [jnp.tanh(it["ps"][:, 3 * w_:3 * w_ + DECAY_LORA]) for it in sel]
            als = [it["ps"][:, 3 * w_ + DECAY_LORA:3 * w_ + DECAY_LORA + AAA_LORA] for it in sel]
            for it, wp, ap in zip(sel, stacked(lambda x: _mm(x, wup_ref[d]), wls),
                                  stacked(lambda x: _mm(x, aup_ref[d]), als)):
                it.update(w_pre=wp, a_pre=ap)
        yield
        for it, nb_ in zip(items, head_bcast([lax.rsqrt(it["nrm"] + 1e-12) for it in items])):
            it["inv_norm"] = nb_
        for it in items:
            d = it["d"]
            logw = -W_DECAY_SCALE * jax.nn.sigmoid(w0_ref[d:d + 1, :] + it["w_pre"])
            tri = ((colm <= row) if it["fwd"] else (colm >= row)).astype(BF16)
            it.update(logw=logw, big_l=_mm_exact_rhs(tri, logw),
                      a=jax.nn.sigmoid(a0_ref[d:d + 1, :] + it["a_pre"]))
        yield
        for it in items:
            bi, fwd, ps, k, a, logw, big_l = (it[x] for x in ("bi", "fwd", "ps", "k", "a", "logw", "big_l"))
            r = ps[:, 0:w_]
            v = ps[:, 2 * w_:3 * w_]
            kk = it["kk0"] * it["inv_norm"]
            kt = k * (1.0 + (a - 1.0) * ka_ref[...])
            b = kk * a
            it.update(v=v, rkt=r * kt * rk_ref[...])
            ltot = big_l[c - 1:c, :] if fwd else big_l[0:1, :]
            e_lx = jnp.exp(big_l - logw)
            e_nl = jnp.exp(-big_l)
            e_tot = jnp.exp(ltot)
            e_rest = e_tot * e_nl
            strict = (jj < trow) if fwd else (jj > trow)
            out.append(dict(bi=bi, d=it["d"], v=v, at=-kk * e_lx, bt=b * e_nl, ktl=kt * e_nl,
                            rt=r * (jnp.exp(big_l) if fwd else e_lx), bh=b * e_rest, kh=kt * e_rest,
                            e_tot=e_tot, strict=strict,
                            omask=(jj <= trow) if fwd else strict))
        fw = [it for it in items if it["fwd"]]
        gls = [jax.nn.sigmoid(it["ps"][:, 3 * w_ + DECAY_LORA + AAA_LORA:]) for it in fw]
        for it, g_ in zip(fw, stacked(lambda x: _mm(x, gup_ref[...]), gls)):
            gate_ref[it["bi"]] = g_
        bsums = head_sums([it["rkt"] for it in fw])
        yield
        for it, bb in zip(fw, head_bcast(bsums)):
            bonus_ref[it["bi"]] = bb * it["v"]

    def chain_stages(ops):
        chains = []
        for po in ops:
            bi, d = po["bi"], po["d"]
            for pr in range(RWKV_GROUPS):
                sl = slice(pr * gl, (pr + 1) * gl)
                lhs = jnp.concatenate([po["at"][:, sl], po["rt"][:, sl]], axis=0)
                m = _mm_nt(lhs, jnp.concatenate([_group_expand(po["bt"][:, sl].astype(BF16), head_of),
                                                 _group_expand(po["ktl"][:, sl].astype(BF16), head_of)], axis=0))
                s_in = st_s[bi, d, pr]
                ch = dict(bi=bi, d=d, pr=pr, sl=sl, v=po["v"][:, sl], bh=po["bh"][:, sl],
                          kh=po["kh"][:, sl], e_tot=po["e_tot"][:, sl], s_in=s_in, sx=_mm_nt(lhs, s_in),
                          a_ab=jnp.where(po["strict"], m[:c, :gl], 0.0),
                          a_ak=jnp.where(po["strict"], m[:c, gl:], 0.0),
                          m_rb=jnp.where(po["omask"], m[c:, :gl], 0.0),
                          m_rk=jnp.where(po["omask"], m[c:, gl:], 0.0))
                chains.append(ch)
        yield

        a_diag = [jnp.where(same_block(INV_BASE_BLOCK), ch["a_ab"], 0.0) for ch in chains]
        tinv = [eye_cat + a for a in a_diag]
        pw = [_mm_group(p, p, head_of) for p in a_diag]
        yield
        span = 4
        while span < INV_BASE_BLOCK:
            both = [_mm_group(jnp.concatenate([t, p], axis=0), p, head_of) for t, p in zip(tinv, pw)]
            tinv = [t + r[:c] for t, r in zip(tinv, both)]
            pw = [r[c:] for r in both]
            yield
            span *= 2
        tinv = [t + _mm_group(t, p, head_of) for t, p in zip(tinv, pw)]
        yield
        size = INV_BASE_BLOCK
        while size < c:
            off = jnp.logical_and(same_block(2 * size), jnp.logical_not(same_block(size)))
            ta = [_mm_group(t, jnp.where(off, ch["a_ab"], 0.0), head_of) for t, ch in zip(tinv, chains)]
            yield
            tinv = [t + _mm_group(x, t, head_of) for t, x in zip(tinv, ta)]
            yield
            size *= 2

        avs = [_mm(jnp.concatenate([ch["a_ak"], ch["m_rk"]], axis=0),
                   _group_expand(ch["v"].astype(BF16), head_of)) for ch in chains]
        yield
        us = [_mm_group(t_m, ch["sx"][:c] + av[:c], head_of) for ch, t_m, av in zip(chains, tinv, avs)]
        yield
        for ch, av, u in zip(chains, avs, us):
            y_ref = yf_ref if ch["d"] == 0 else yb_ref
            y_ref[ch["bi"], :, ch["sl"]] = ch["sx"][c:] + av[c:] + _mm_group(ch["m_rb"], u, head_of)
        upds = [_mm_tn(jnp.concatenate([u, ch["v"]], axis=0), jnp.concatenate([ch["bh"], ch["kh"]], axis=0))
                for ch, u in zip(chains, us)]
        yield
        for ch, upd in zip(chains, upds):
            st_s[ch["bi"], ch["d"], ch["pr"]] = ch["s_in"] * ch["e_tot"] + jnp.where(blockdiag, upd, 0.0)

    groups = [range(g, min(g + RWKV_CHAIN_ROWS, nb)) for g in range(0, nb, RWKV_CHAIN_ROWS)]
    ops = []
    for _ in prep_stages(groups[0], ops):
        pass
    for gi in range(len(groups)):
        nxt = []
        prep = prep_stages(groups[gi + 1], nxt) if gi + 1 < len(groups) else iter(())
        for lvl, _ in enumerate(chain_stages(ops)):
            if lvl % PREP_EVERY_CHAIN_STAGES == 0:
                next(prep, None)
        for _ in prep:
            pass
        ops = nxt

    @pl.when(j == n - 1)
    def _():
        for p in range(fin_ref.shape[2]):
            fin_ref[:, :, p] = pair_window(st_s, p)[...]


def _rwkv_scan(p_rw, s0, w0, wup, a0, aup, gup, k_k, k_a, r_k, red, bc2):
    bsz, t, cols = p_rw.shape
    c = RWKV_CHUNK
    nb = RWKV_BATCH_BLOCK
    n = t // c
    const2 = lambda b, i: (0, 0)
    const3 = lambda b, i: (0, 0, 0)
    state_blk = (nb, 2, RWKV_STATE_PAIRS, LANES, LANES)
    state_map = lambda b, i: (b, 0, 0, 0, 0)
    state_shape = jax.ShapeDtypeStruct((bsz, 2, RWKV_STATE_PAIRS, LANES, LANES), F32)
    tok_shape = jax.ShapeDtypeStruct((bsz, t, RWKV_WIDTH), F32)
    tok_f = pl.BlockSpec((nb, c, RWKV_WIDTH), lambda b, i: (b, i, 0))
    tok_b = pl.BlockSpec((nb, c, RWKV_WIDTH), lambda b, i: (b, n - 1 - i, 0))
    return pl.pallas_call(
        _rwkv_scan_kernel,
        grid=(bsz // nb, n),
        in_specs=[
            pl.BlockSpec((nb, c, cols), lambda b, i: (b, i, 0)),
            pl.BlockSpec((nb, c, cols), lambda b, i: (b, n - 1 - i, 0)),
            pl.BlockSpec(w0.shape, const2),
            pl.BlockSpec(wup.shape, const3), pl.BlockSpec(a0.shape, const2),
            pl.BlockSpec(aup.shape, const3), pl.BlockSpec(gup.shape, const2),
            pl.BlockSpec(k_k.shape, const2), pl.BlockSpec(k_a.shape, const2),
            pl.BlockSpec(r_k.shape, const2), pl.BlockSpec(red.shape, const2),
            pl.BlockSpec(bc2.shape, const2),
            pl.BlockSpec(state_blk, state_map, pipeline_mode=pl.Buffered(1))],
        out_specs=[tok_f, tok_b, tok_f, tok_f, pl.BlockSpec(state_blk, state_map)],
        out_shape=[tok_shape] * 4 + [state_shape],
        scratch_shapes=[pltpu.VMEM((nb, 2, RWKV_GROUPS, RWKV_GROUP_LANES, RWKV_GROUP_LANES), F32)],
        compiler_params=_params("parallel", "arbitrary"),
        name="rwkv_scan",
    )(p_rw, p_rw, w0, wup, a0, aup, gup, k_k, k_a, r_k, red, bc2, s0)


def _mlp_kernel(x_ref, ret_ref, yf_ref, yb_ref, bonus_ref, gate_ref, lnw_ref, lnb_ref, red_ref, bc_ref,
                g1_ref, sh2_ref, sc2_ref, g2_ref, n2g_ref, fg_ref,
                woa_ref, wob_ref, w1_ref, b1_ref, w2_ref, b2_ref, o_ref, *, ff_tile):
    mix_ret = jnp.dot(ret_ref[0], woa_ref[...], preferred_element_type=F32)
    red = red_ref[...]
    bc2 = bc_ref[...]
    inv_n = 1.0 / RWKV_HEAD_DIM
    blk = MERGE_ROW_BLOCK
    rows = [slice(r, r + blk) for r in range(0, x_ref.shape[1], blk)]

    def head_means(xs):
        ss = [jnp.dot(jnp.concatenate(_split2(x), axis=0), red, preferred_element_type=F32) for x in xs]
        return [jnp.dot(jnp.concatenate(_split2(s[:blk] + s[blk:]), axis=1), bc2,
                        preferred_element_type=F32) * inv_n for s in ss]

    ys = [yf_ref[0, r, :] + yb_ref[0, r, :] for r in rows]
    ycs = [y - m for y, m in zip(ys, head_means(ys))]
    variances = head_means([yc * yc for yc in ycs])
    rw = jnp.concatenate(
        [((yc * lax.rsqrt(var + GN_EPS) * lnw_ref[...] + lnb_ref[...] + bonus_ref[0, r, :])
          * gate_ref[0, r, :]).astype(BF16) for yc, var, r in zip(ycs, variances, rows)], axis=0)
    mix = mix_ret + jnp.dot(rw, wob_ref[...], preferred_element_type=F32)
    h1 = x_ref[0] + g1_ref[0] * mix
    n2 = (_rms(h1, n2g_ref[...] * (1.0 + sc2_ref[0])) + sh2_ref[0]).astype(BF16)
    d_ff = w1_ref.shape[1]
    slices = [slice(s * ff_tile, (s + 1) * ff_tile) for s in range(d_ff // ff_tile)]

    def up(sl):
        return jnp.dot(n2, w1_ref[:, sl], preferred_element_type=F32) + b1_ref[:, sl]

    acc = jnp.zeros(h1.shape, F32)
    hid = up(slices[0])
    for s, sl in enumerate(slices):
        nxt = up(slices[s + 1]) if s + 1 < len(slices) else None
        act = jnp.square(jnp.maximum(hid, 0.0)).astype(BF16)
        acc = acc + jnp.dot(act, w2_ref[sl, :], preferred_element_type=F32)
        hid = nxt
    h2 = h1 + g2_ref[0] * (acc + b2_ref[...])
    o_ref[0] = _rms(h2, fg_ref[...])


def _mlp(x, ret, yf, yb, bonus, gate, ln_w, ln_b, red, bc2, g1, sh2, sc2, g2, n2g, fg,
         woa, wob, w1, b1, w2, b2, tm):
    bsz, t, d = x.shape
    const = lambda b, i: (0, 0)
    single = pl.Buffered(1)
    mod = pl.BlockSpec((1, 1, d), lambda b, i: (b, 0, 0))
    cw = lambda a: pl.BlockSpec(a.shape, const, pipeline_mode=single)
    tok = lambda w: pl.BlockSpec((1, tm, w), lambda b, i: (b, i, 0))
    return pl.pallas_call(
        functools.partial(_mlp_kernel, ff_tile=1024),
        grid=(bsz, t // tm),
        in_specs=[tok(d), tok(RET_WIDTH), tok(RWKV_WIDTH), tok(RWKV_WIDTH), tok(RWKV_WIDTH),
                  tok(RWKV_WIDTH), cw(ln_w), cw(ln_b), cw(red), cw(bc2),
                  mod, mod, mod, mod, cw(n2g), cw(fg),
                  cw(woa), cw(wob), cw(w1), cw(b1), cw(w2), cw(b2)],
        out_specs=tok(d),
        out_shape=jax.ShapeDtypeStruct((bsz, t, d), F32),
        compiler_params=_params("parallel", "parallel"),
        name="outproj_mlp",
    )(x, ret, yf, yb, bonus, gate, ln_w, ln_b, red, bc2, g1, sh2, sc2, g2, n2g, fg,
      woa, wob, w1, b1, w2, b2)


def _rope_tables(t):
    half = RET_HEAD_DIM // 2
    inv = jnp.power(ROPE_BASE, -jnp.arange(0, half, 2, dtype=F32) / half)
    pos = jnp.arange(t)
    ang_r = (pos // GRID_W).astype(F32)[:, None] * inv[None, :]
    ang_c = (pos % GRID_W).astype(F32)[:, None] * inv[None, :]
    cos_t = jnp.concatenate([jnp.cos(ang_r)] * 2 + [jnp.cos(ang_c)] * 2, axis=-1)
    sin_t = jnp.concatenate([-jnp.sin(ang_r), jnp.sin(ang_r), -jnp.sin(ang_c), jnp.sin(ang_c)], axis=-1)
    return cos_t, sin_t


def kernel(x, c, ctx, c_ctx, w_ada, b_ada, norm1_g, norm2_g, w_in, ret_log_decay, rwkv_shift_mu,
           rwkv_w0, rwkv_w_up, rwkv_a0, rwkv_a_up, rwkv_g_up, rwkv_k_k, rwkv_k_a, rwkv_r_k,
           rwkv_ln_w, rwkv_ln_b, w_out, w_ff1, b_ff1, w_ff2, b_ff2, final_g):
    bsz, t, d = x.shape
    assert w_ada.shape[0] == 1, "single trunk layer"
    assert bsz % RWKV_BATCH_BLOCK == 0
    l = 0

    rows = -(-(bsz + 1) // SUBLANES) * SUBLANES
    cc = jnp.zeros((rows, d), F32).at[:bsz].set(c).at[bsz].set(c_ctx)
    mods = _ada(cc, w_ada[l], b_ada[l][None, :])
    sh1, sc1, g1, sh2, sc2, g2 = [m[:bsz, None, :] for m in jnp.split(mods, 6, axis=-1)]
    csh1, csc1 = [jnp.broadcast_to(m[bsz][None, None, :], (bsz, 1, d))
                  for m in jnp.split(mods, 6, axis=-1)[:2]]

    w_in_b = w_in[l].astype(BF16)
    w_ret, w_rw = w_in_b[:, :RET_COLS], w_in_b[:, RET_COLS:]
    n1g = norm1_g[l][None, :]
    mu = rwkv_shift_mu[l]
    p_ret, p_rw = _inproj(x, sh1, sc1, n1g, w_ret, w_rw, mu, tm=TOKEN_TILE)
    pc_ret, pc_rw = _inproj(ctx, csh1, csc1, n1g, w_ret, w_rw, mu, tm=ctx.shape[1])

    cos_t, sin_t = _rope_tables(t)
    ld = jnp.broadcast_to(ret_log_decay[l][:, :, None, None], (2, RET_HEADS, SUBLANES, LANES))
    ret_out = _retention(p_ret, pc_ret, cos_t, sin_t, ld)

    head_id = jnp.arange(RWKV_WIDTH) // RWKV_HEAD_DIM
    red = (head_id[:, None] == jnp.arange(LANES)[None, :]).astype(BF16)
    bc2 = jnp.concatenate([red.T, red.T], axis=0)
    rw_params = (rwkv_w0[l], rwkv_w_up[l].astype(BF16), rwkv_a0[l],
                 rwkv_a_up[l].astype(BF16), rwkv_g_up[l].astype(BF16),
                 rwkv_k_k[l][None, :], rwkv_k_a[l][None, :], rwkv_r_k[l][None, :], red, bc2)
    zeros = jnp.zeros((bsz, 2, RWKV_STATE_PAIRS, LANES, LANES), F32)
    s_ctx = _rwkv_scan(pc_rw, zeros, *rw_params)[4]
    yf, yb, bonus, gate, _ = _rwkv_scan(p_rw, s_ctx, *rw_params)

    wo = w_out[l].astype(BF16)
    return _mlp(x, ret_out, yf, yb, bonus, gate, rwkv_ln_w[l][None, :], rwkv_ln_b[l][None, :], red, bc2,
                g1, sh2, sc2, g2, norm2_g[l][None, :], final_g[None, :],
                wo[:RET_WIDTH], wo[RET_WIDTH:], w_ff1[l].astype(BF16), b_ff1[l][None, :],
                w_ff2[l].astype(BF16), b_ff2[l][None, :], tm=TOKEN_TILE)
```

```python
import functools
import math

import jax
import jax.numpy as jnp
from jax import lax
from jax.experimental import pallas as pl
from jax.experimental.pallas import tpu as pltpu

F32 = jnp.float32
BF16 = jnp.bfloat16

GRID_W = 64
RET_HEADS = 4
RET_HEAD_DIM = 128
RET_WIDTH = RET_HEADS * RET_HEAD_DIM
RET_CHUNK = 128
RWKV_HEAD_DIM = 64
RWKV_WIDTH = 512
RWKV_GROUP_HEADS = 2
RWKV_GROUP_LANES = RWKV_GROUP_HEADS * RWKV_HEAD_DIM
RWKV_GROUPS = RWKV_WIDTH // RWKV_GROUP_LANES
RWKV_STATE_PAIRS = RWKV_WIDTH // 128
RWKV_CHUNK = 64
RWKV_BATCH_BLOCK = 8
RWKV_CHAIN_ROWS = 2
PREP_EVERY_CHAIN_STAGES = 3
INV_BASE_BLOCK = 16
DECAY_LORA = 64
AAA_LORA = 64
GATE_LORA = 128
SHIFT_COLS = 3 * RWKV_WIDTH + DECAY_LORA + AAA_LORA + GATE_LORA
RET_COLS = 4 * RET_WIDTH
ROPE_BASE = 10000.0
NORM_EPS = 1e-6
GN_EPS = 64e-5
W_DECAY_SCALE = math.exp(-0.5)
LANES = 128
SUBLANES = 8
TOKEN_TILE = 512
MERGE_ROW_BLOCK = 128
VMEM_LIMIT = 56 * 1024 * 1024


def _params(*sem):
    return pltpu.CompilerParams(dimension_semantics=sem, vmem_limit_bytes=VMEM_LIMIT)


def _mm(a, b):
    return jnp.dot(a.astype(BF16), b.astype(BF16), preferred_element_type=F32)


def _mm_nt(a, b):
    return lax.dot_general(a.astype(BF16), b.astype(BF16), (((1,), (1,)), ((), ())),
                           preferred_element_type=F32)


def _mm_tn(a, b):
    return lax.dot_general(a.astype(BF16), b.astype(BF16), (((0,), (0,)), ((), ())),
                           preferred_element_type=F32)


def _split3(x):
    hi = x.astype(BF16)
    r1 = x - hi.astype(F32)
    mid = r1.astype(BF16)
    lo = (r1 - mid.astype(F32)).astype(BF16)
    return hi, mid, lo


def _mm_exact_rhs(a_bf16, x):
    return jnp.dot(jnp.concatenate([a_bf16] * 3, axis=1), jnp.concatenate(_split3(x), axis=0),
                   preferred_element_type=F32)


def _split2(x):
    hi = x.astype(BF16)
    return hi, (x - hi.astype(F32)).astype(BF16)


def _rms(x, g):
    return x * lax.rsqrt(jnp.mean(x * x, axis=-1, keepdims=True) + NORM_EPS) * g


def _ada_kernel(c_ref, w_ref, b_ref, o_ref):
    s = jax.nn.silu(c_ref[...])
    o_ref[...] = jnp.dot(s, w_ref[...], preferred_element_type=F32,
                         precision=lax.Precision.HIGHEST) + b_ref[...]


def _ada(cc, w, b):
    rows, d = cc.shape
    n = w.shape[1]
    tn = n // 4
    return pl.pallas_call(
        _ada_kernel,
        grid=(n // tn,),
        in_specs=[pl.BlockSpec((rows, d), lambda j: (0, 0)),
                  pl.BlockSpec((d, tn), lambda j: (0, j)),
                  pl.BlockSpec((1, tn), lambda j: (0, j))],
        out_specs=pl.BlockSpec((rows, tn), lambda j: (0, j)),
        out_shape=jax.ShapeDtypeStruct((rows, n), F32),
        compiler_params=_params("arbitrary"),
        name="adaln",
    )(cc, w, b)


def _inproj_kernel(x_ref, xp_ref, xn_ref, sh_ref, sc_ref, g_ref, wr_ref, ww_ref, mu_ref,
                   pret_ref, prw_ref):
    i = pl.program_id(1)
    n = pl.num_programs(1)
    tm = x_ref.shape[1]
    halo = SUBLANES

    gain = g_ref[...] * (1.0 + sc_ref[0])

    def modulated(xv):
        return xv * lax.rsqrt(jnp.mean(xv * xv, axis=-1, keepdims=True) + NORM_EPS) * gain + sh_ref[0]

    h = modulated(x_ref[0])
    h_ext = jnp.concatenate([modulated(xp_ref[0]), h, modulated(xn_ref[0])], axis=0).astype(BF16)
    p_ext = jnp.dot(h_ext, ww_ref[...], preferred_element_type=F32)
    rid = lax.broadcasted_iota(jnp.int32, (tm + 2 * halo, 1), 0)
    inside = jnp.logical_and(jnp.logical_or(rid >= halo, i > 0),
                             jnp.logical_or(rid < tm + halo, i < n - 1))
    ext = jnp.where(inside, p_ext, 0.0)
    pret_ref[0] = jnp.dot(h.astype(BF16), wr_ref[...], preferred_element_type=F32)
    rows = tm + 2 * halo
    p = ext[halo:halo + tm]
    prev = pltpu.roll(ext, 1, axis=0)[halo:halo + tm]
    nxt = pltpu.roll(ext, rows - 1, axis=0)[halo:halo + tm]
    mu_prev, mu_next = mu_ref[0:1, :], mu_ref[1:2, :]
    prw_ref[0] = (1.0 - mu_prev - mu_next) * p + mu_prev * prev + mu_next * nxt


def _inproj(x, shift, scale, g, w_ret, w_rw, mu, tm):
    bsz, t, d = x.shape
    const = lambda b, i: (0, 0)
    blk8 = tm // SUBLANES
    last8 = t // SUBLANES - 1
    return pl.pallas_call(
        _inproj_kernel,
        grid=(bsz, t // tm),
        in_specs=[pl.BlockSpec((1, tm, d), lambda b, i: (b, i, 0)),
                  pl.BlockSpec((1, SUBLANES, d), lambda b, i: (b, jnp.maximum(i * blk8 - 1, 0), 0)),
                  pl.BlockSpec((1, SUBLANES, d), lambda b, i: (b, jnp.minimum((i + 1) * blk8, last8), 0)),
                  pl.BlockSpec((1, 1, d), lambda b, i: (b, 0, 0)),
                  pl.BlockSpec((1, 1, d), lambda b, i: (b, 0, 0)),
                  pl.BlockSpec((1, d), const),
                  pl.BlockSpec(w_ret.shape, const, pipeline_mode=pl.Buffered(1)),
                  pl.BlockSpec(w_rw.shape, const, pipeline_mode=pl.Buffered(1)),
                  pl.BlockSpec(mu.shape, const)],
        out_specs=[pl.BlockSpec((1, tm, RET_COLS), lambda b, i: (b, i, 0)),
                   pl.BlockSpec((1, tm, SHIFT_COLS), lambda b, i: (b, i, 0))],
        out_shape=[jax.ShapeDtypeStruct((bsz, t, RET_COLS), F32),
                   jax.ShapeDtypeStruct((bsz, t, SHIFT_COLS), F32)],
        compiler_params=_params("parallel", "parallel"),
        name="inproj",
    )(x, x, x, shift, scale, g, w_ret, w_rw, mu)


def _ret_kernel(q_ref, k_ref, v_ref, g_ref, kc_ref, vc_ref, cos_ref, sin_ref, ld_ref, o_ref):
    c = RET_CHUNK
    hd = RET_HEAD_DIM
    t = q_ref.shape[1]
    tc = kc_ref.shape[1]
    n = t // c
    k_scale = hd ** -0.5
    lgf = -jnp.exp(ld_ref[0, 0][0:1, :])
    lgb = -jnp.exp(ld_ref[1, 0][0:1, :])

    jc = lax.broadcasted_iota(jnp.int32, (tc, 1), 0).astype(F32)
    kc = kc_ref[0] * k_scale
    vc = vc_ref[0]
    s_f = _mm_tn(kc * jnp.exp(lgf * (tc - 1.0 - jc)), vc)
    s_b = _mm_tn(kc * jnp.exp(lgb * jc), vc)

    lane = lax.broadcasted_iota(jnp.int32, (c, hd), 1)
    first_half = (lane % 64) < 32

    def rope(x, cs, sn):
        swapped = jnp.where(first_half, pltpu.roll(x, 96, axis=1), pltpu.roll(x, 32, axis=1))
        return x * cs + swapped * sn

    ii = lax.broadcasted_iota(jnp.int32, (c, 1), 0).astype(F32)
    dist = (lax.broadcasted_iota(jnp.int32, (c, c), 0)
            - lax.broadcasted_iota(jnp.int32, (c, c), 1)).astype(F32)
    decay = jnp.where(dist >= 0, jnp.exp(lgf * jnp.maximum(dist, 0.0)),
                      jnp.exp(lgb * jnp.maximum(-dist, 0.0)))
    qdec_f = jnp.exp(lgf * (ii + 1.0))
    kdec_f = jnp.exp(lgf * (c - 1.0 - ii))
    qdec_b = jnp.exp(lgb * (c - ii))
    kdec_b = jnp.exp(lgb * ii)
    cdec_f = jnp.exp(lgf * c)
    cdec_b = jnp.exp(lgb * c)

    chunks = [slice(ci * c, (ci + 1) * c) for ci in range(n)]
    qs = [rope(q_ref[0, sl, :], cos_ref[sl, :], sin_ref[sl, :]) for sl in chunks]
    ks = [rope(k_ref[0, sl, :] * k_scale, cos_ref[sl, :], sin_ref[sl, :]) for sl in chunks]
    scores = [_mm_nt(qc, kx) for qc, kx in zip(qs, ks)]
    kvs = [_mm_tn(jnp.concatenate([kx * kdec_f, kx * kdec_b], axis=1), v_ref[0, sl, :])
           for kx, sl in zip(ks, chunks)]

    sfs, sbs = [None] * n, [None] * n
    for ci in range(n):
        sfs[ci] = s_f
        s_f = s_f * cdec_f + kvs[ci][:hd]
    for ci in range(n - 1, -1, -1):
        sbs[ci] = s_b
        s_b = s_b * cdec_b + kvs[ci][hd:]

    outs = [_mm(jnp.concatenate([sc * decay, qc * qdec_f, qc * qdec_b], axis=1),
                jnp.concatenate([v_ref[0, sl, :].astype(BF16), sf.astype(BF16), sb.astype(BF16)], axis=0))
            for sc, qc, sl, sf, sb in zip(scores, qs, chunks, sfs, sbs)]
    for o, sl in zip(outs, chunks):
        o = o * lax.rsqrt(jnp.mean(o * o, axis=-1, keepdims=True) + NORM_EPS)
        o_ref[0, sl, :] = (o * jax.nn.silu(g_ref[0, sl, :])).astype(o_ref.dtype)


def _retention(p_ret, pc_ret, cos_t, sin_t, ld):
    bsz, t, _ = p_ret.shape
    tc = pc_ret.shape[1]
    hd = RET_HEAD_DIM
    col = lambda j: pl.BlockSpec((1, t, hd), lambda b, h: (b, 0, h + RET_HEADS * j))
    colc = lambda j: pl.BlockSpec((1, tc, hd), lambda b, h: (b, 0, h + RET_HEADS * j))
    return pl.pallas_call(
        _ret_kernel,
        grid=(bsz, RET_HEADS),
        in_specs=[col(0), col(1), col(2), col(3), colc(1), colc(2),
                  pl.BlockSpec((t, hd), lambda b, h: (0, 0)),
                  pl.BlockSpec((t, hd), lambda b, h: (0, 0)),
                  pl.BlockSpec((2, 1, SUBLANES, LANES), lambda b, h: (0, h, 0, 0))],
        out_specs=pl.BlockSpec((1, t, hd), lambda b, h: (b, 0, h)),
        out_shape=jax.ShapeDtypeStruct((bsz, t, RET_WIDTH), BF16),
        compiler_params=_params("parallel", "parallel"),
        name="retention",
    )(p_ret, p_ret, p_ret, p_ret, pc_ret, pc_ret, cos_t, sin_t, ld)


def _group_expand(y, head_of_lane):
    zero = jnp.zeros_like(y)
    return jnp.concatenate([jnp.where(head_of_lane == h, y, zero) for h in range(RWKV_GROUP_HEADS)],
                           axis=0)


def _mm_group(a, y, head_of_lane):
    return _mm(a, _group_expand(y.astype(BF16), head_of_lane))


def _rwkv_scan_kernel(pf_ref, pb_ref, w0_ref, wup_ref, a0_ref, aup_ref, gup_ref, kk_ref, ka_ref, rk_ref,
                      red_ref, bc_ref, s0_ref, yf_ref, yb_ref, bonus_ref, gate_ref, fin_ref, st_s):
    c = RWKV_CHUNK
    w_ = RWKV_WIDTH
    hd = RWKV_HEAD_DIM
    nb = pf_ref.shape[0]
    j = pl.program_id(1)
    n = pl.num_programs(1)

    per_group = RWKV_GROUP_LANES // LANES

    def pair_window(ref, p):
        g, o = divmod(p, per_group)
        return ref.at[:, :, g, o * LANES:(o + 1) * LANES, o * LANES:(o + 1) * LANES]

    @pl.when(j == 0)
    def _():
        st_s[...] = jnp.zeros(st_s.shape, F32)
        for p in range(s0_ref.shape[2]):
            pair_window(st_s, p)[...] = s0_ref[:, :, p]

    row = lax.broadcasted_iota(jnp.int32, (c, c), 0)
    colm = lax.broadcasted_iota(jnp.int32, (c, c), 1)
    gl = RWKV_GROUP_LANES
    lane = lax.broadcasted_iota(jnp.int32, (c, gl), 1)
    trow = lax.broadcasted_iota(jnp.int32, (c, gl), 0)
    head_of = lane // hd
    jj = lane % hd
    r2 = lax.broadcasted_iota(jnp.int32, (gl, gl), 0)
    c2 = lax.broadcasted_iota(jnp.int32, (gl, gl), 1)
    blockdiag = (r2 // hd) == (c2 // hd)
    eye_cat = jnp.where(jj == trow, 1.0, 0.0)
    red = red_ref[...]
    bc2 = bc_ref[...]

    def same_block(size):
        return (trow // size) == (jj // size)

    def stacked(fn, xs):
        out = fn(jnp.concatenate(xs, axis=0))
        return [out[i * c:(i + 1) * c] for i in range(len(xs))]

    def head_sums(xs):
        return stacked(lambda x: jnp.dot(x.astype(BF16), red, preferred_element_type=F32), xs)

    def head_bcast(ss):
        return stacked(lambda s: jnp.dot(jnp.concatenate(_split2(s), axis=1), bc2,
                                         preferred_element_type=F32), ss)

    def prep_stages(rows, out):
        items = [dict(bi=bi, d=d, fwd=d == 0) for bi in rows for d in range(2)]
        for it in items:
            ps = (pf_ref if it["fwd"] else pb_ref)[it["bi"]]
            k = ps[:, w_:2 * w_]
            it.update(ps=ps, k=k, kk0=k * kk_ref[...])
        for it, s in zip(items, head_sums([it["kk0"] * it["kk0"] for it in items])):
            it["nrm"] = s
        for d in range(2):
            sel = [it for it in items if it["d"] == d]
            wls = [jnp.tanh(it["ps"][:, 3 * w_:3 * w_ + DECAY_LORA]) for it in sel]
            als = [it["ps"][:, 3 * w_ + DECAY_LORA:3 * w_ + DECAY_LORA + AAA_LORA] for it in sel]
            for it, wp, ap in zip(sel, stacked(lambda x: _mm(x, wup_ref[d]), wls),
                                  stacked(lambda x: _mm(x, aup_ref[d]), als)):
                it.update(w_pre=wp, a_pre=ap)
        yield
        for it, nb_ in zip(items, head_bcast([lax.rsqrt(it["nrm"] + 1e-12) for it in items])):
            it["inv_norm"] = nb_
        for it in items:
            d = it["d"]
            logw = -W_DECAY_SCALE * jax.nn.sigmoid(w0_ref[d:d + 1, :] + it["w_pre"])
            tri = ((colm <= row) if it["fwd"] else (colm >= row)).astype(BF16)
            it.update(logw=logw, big_l=_mm_exact_rhs(tri, logw),
                      a=jax.nn.sigmoid(a0_ref[d:d + 1, :] + it["a_pre"]))
        yield
        for it in items:
            bi, fwd, ps, k, a, logw, big_l = (it[x] for x in ("bi", "fwd", "ps", "k", "a", "logw", "big_l"))
            r = ps[:, 0:w_]
            v = ps[:, 2 * w_:3 * w_]
            kk = it["kk0"] * it["inv_norm"]
            kt = k * (1.0 + (a - 1.0) * ka_ref[...])
            b = kk * a
            it.update(v=v, rkt=r * kt * rk_ref[...])
            ltot = big_l[c - 1:c, :] if fwd else big_l[0:1, :]
            e_lx = jnp.exp(big_l - logw)
            e_nl = jnp.exp(-big_l)
            e_tot = jnp.exp(ltot)
            e_rest = e_tot * e_nl
            strict = (jj < trow) if fwd else (jj > trow)
            out.append(dict(bi=bi, d=it["d"], v=v, at=-kk * e_lx, bt=b * e_nl, ktl=kt * e_nl,
                            rt=r * (jnp.exp(big_l) if fwd else e_lx), bh=b * e_rest, kh=kt * e_rest,
                            e_tot=e_tot, strict=strict,
                            omask=(jj <= trow) if fwd else strict))
        fw = [it for it in items if it["fwd"]]
        gls = [jax.nn.sigmoid(it["ps"][:, 3 * w_ + DECAY_LORA + AAA_LORA:]) for it in fw]
        for it, g_ in zip(fw, stacked(lambda x: _mm(x, gup_ref[...]), gls)):
            gate_ref[it["bi"]] = g_
        bsums = head_sums([it["rkt"] for it in fw])
        yield
        for it, bb in zip(fw, head_bcast(bsums)):
            bonus_ref[it["bi"]] = bb * it["v"]

    def chain_stages(ops):
        chains = []
        for po in ops:
            bi, d = po["bi"], po["d"]
            for pr in range(RWKV_GROUPS):
                sl = slice(pr * gl, (pr + 1) * gl)
                lhs = jnp.concatenate([po["at"][:, sl], po["rt"][:, sl]], axis=0)
                m = _mm_nt(lhs, jnp.concatenate([_group_expand(po["bt"][:, sl].astype(BF16), head_of),
                                                 _group_expand(po["ktl"][:, sl].astype(BF16), head_of)], axis=0))
                s_in = st_s[bi, d, pr]
                ch = dict(bi=bi, d=d, pr=pr, sl=sl, v=po["v"][:, sl], bh=po["bh"][:, sl],
                          kh=po["kh"][:, sl], e_tot=po["e_tot"][:, sl], s_in=s_in, sx=_mm_nt(lhs, s_in),
                          a_ab=jnp.where(po["strict"], m[:c, :gl], 0.0),
                          a_ak=jnp.where(po["strict"], m[:c, gl:], 0.0),
                          m_rb=jnp.where(po["omask"], m[c:, :gl], 0.0),
                          m_rk=jnp.where(po["omask"], m[c:, gl:], 0.0))
                chains.append(ch)
        yield

        a_diag = [jnp.where(same_block(INV_BASE_BLOCK), ch["a_ab"], 0.0) for ch in chains]
        tinv = [eye_cat + a for a in a_diag]
        pw = [_mm_group(p, p, head_of) for p in a_diag]
        yield
        span = 4
        while span < INV_BASE_BLOCK:
            both = [_mm_group(jnp.concatenate([t, p], axis=0), p, head_of) for t, p in zip(tinv, pw)]
            tinv = [t + r[:c] for t, r in zip(tinv, both)]
            pw = [r[c:] for r in both]
            yield
            span *= 2
        tinv = [t + _mm_group(t, p, head_of) for t, p in zip(tinv, pw)]
        yield
        size = INV_BASE_BLOCK
        while size < c:
            off = jnp.logical_and(same_block(2 * size), jnp.logical_not(same_block(size)))
            ta = [_mm_group(t, jnp.where(off, ch["a_ab"], 0.0), head_of) for t, ch in zip(tinv, chains)]
            yield
            tinv = [t + _mm_group(x, t, head_of) for t, x in zip(tinv, ta)]
            yield
            size *= 2

        avs = [_mm(jnp.concatenate([ch["a_ak"], ch["m_rk"]], axis=0),
                   _group_expand(ch["v"].astype(BF16), head_of)) for ch in chains]
        yield
        us = [_mm_group(t_m, ch["sx"][:c] + av[:c], head_of) for ch, t_m, av in zip(chains, tinv, avs)]
        yield
        for ch, av, u in zip(chains, avs, us):
            y_ref = yf_ref if ch["d"] == 0 else yb_ref
            y_ref[ch["bi"], :, ch["sl"]] = ch["sx"][c:] + av[c:] + _mm_group(ch["m_rb"], u, head_of)
        upds = [_mm_tn(jnp.concatenate([u, ch["v"]], axis=0), jnp.concatenate([ch["bh"], ch["kh"]], axis=0))
                for ch, u in zip(chains, us)]
        yield
        for ch, upd in zip(chains, upds):
            st_s[ch["bi"], ch["d"], ch["pr"]] = ch["s_in"] * ch["e_tot"] + jnp.where(blockdiag, upd, 0.0)

    groups = [range(g, min(g + RWKV_CHAIN_ROWS, nb)) for g in range(0, nb, RWKV_CHAIN_ROWS)]
    ops = []
    for _ in prep_stages(groups[0], ops):
        pass
    for gi in range(len(groups)):
        nxt = []
        prep = prep_stages(groups[gi + 1], nxt) if gi + 1 < len(groups) else iter(())
        for lvl, _ in enumerate(chain_stages(ops)):
            if lvl % PREP_EVERY_CHAIN_STAGES == 0:
                next(prep, None)
        for _ in prep:
            pass
        ops = nxt

    @pl.when(j == n - 1)
    def _():
        for p in range(fin_ref.shape[2]):
            fin_ref[:, :, p] = pair_window(st_s, p)[...]


def _rwkv_scan(p_rw, s0, w0, wup, a0, aup, gup, k_k, k_a, r_k, red, bc2):
    bsz, t, cols = p_rw.shape
    c = RWKV_CHUNK
    nb = RWKV_BATCH_BLOCK
    n = t // c
    const2 = lambda b, i: (0, 0)
    const3 = lambda b, i: (0, 0, 0)
    state_blk = (nb, 2, RWKV_STATE_PAIRS, LANES, LANES)
    state_map = lambda b, i: (b, 0, 0, 0, 0)
    state_shape = jax.ShapeDtypeStruct((bsz, 2, RWKV_STATE_PAIRS, LANES, LANES), F32)
    tok_shape = jax.ShapeDtypeStruct((bsz, t, RWKV_WIDTH), F32)
    tok_f = pl.BlockSpec((nb, c, RWKV_WIDTH), lambda b, i: (b, i, 0))
    tok_b = pl.BlockSpec((nb, c, RWKV_WIDTH), lambda b, i: (b, n - 1 - i, 0))
    return pl.pallas_call(
        _rwkv_scan_kernel,
        grid=(bsz // nb, n),
        in_specs=[
            pl.BlockSpec((nb, c, cols), lambda b, i: (b, i, 0)),
            pl.BlockSpec((nb, c, cols), lambda b, i: (b, n - 1 - i, 0)),
            pl.BlockSpec(w0.shape, const2),
            pl.BlockSpec(wup.shape, const3), pl.BlockSpec(a0.shape, const2),
            pl.BlockSpec(aup.shape, const3), pl.BlockSpec(gup.shape, const2),
            pl.BlockSpec(k_k.shape, const2), pl.BlockSpec(k_a.shape, const2),
            pl.BlockSpec(r_k.shape, const2), pl.BlockSpec(red.shape, const2),
            pl.BlockSpec(bc2.shape, const2),
            pl.BlockSpec(state_blk, state_map, pipeline_mode=pl.Buffered(1))],
        out_specs=[tok_f, tok_b, tok_f, tok_f, pl.BlockSpec(state_blk, state_map)],
        out_shape=[tok_shape] * 4 + [state_shape],
        scratch_shapes=[pltpu.VMEM((nb, 2, RWKV_GROUPS, RWKV_GROUP_LANES, RWKV_GROUP_LANES), F32)],
        compiler_params=_params("parallel", "arbitrary"),
        name="rwkv_scan",
    )(p_rw, p_rw, w0, wup, a0, aup, gup, k_k, k_a, r_k, red, bc2, s0)


def _mlp_kernel(x_ref, ret_ref, yf_ref, yb_ref, bonus_ref, gate_ref, lnw_ref, lnb_ref, red_ref, bc_ref,
                g1_ref, sh2_ref, sc2_ref, g2_ref, n2g_ref, fg_ref,
                woa_ref, wob_ref, w1_ref, b1_ref, w2_ref, b2_ref, o_ref, *, ff_tile):
    mix_ret = jnp.dot(ret_ref[0], woa_ref[...], preferred_element_type=F32)
    red = red_ref[...]
    bc2 = bc_ref[...]
    inv_n = 1.0 / RWKV_HEAD_DIM
    blk = MERGE_ROW_BLOCK
    rows = [slice(r, r + blk) for r in range(0, x_ref.shape[1], blk)]

    def head_means(xs, two_piece):
        if two_piece:
            ss = [jnp.dot(jnp.concatenate(_split2(x), axis=0), red, preferred_element_type=F32) for x in xs]
            ss = [s[:blk] + s[blk:] for s in ss]
        else:
            ss = [jnp.dot(x.astype(BF16), red, preferred_element_type=F32) for x in xs]
        return [jnp.dot(jnp.concatenate(_split2(s * inv_n), axis=1), bc2, preferred_element_type=F32)
                for s in ss]

    ys = [yf_ref[0, r, :] + yb_ref[0, r, :] for r in rows]
    ycs = [y - m for y, m in zip(ys, head_means(ys, two_piece=True))]
    variances = head_means([yc * yc for yc in ycs], two_piece=False)
    rw = jnp.concatenate(
        [((yc * lax.rsqrt(var + GN_EPS) * lnw_ref[...] + lnb_ref[...] + bonus_ref[0, r, :])
          * gate_ref[0, r, :]).astype(BF16) for yc, var, r in zip(ycs, variances, rows)], axis=0)
    mix = mix_ret + jnp.dot(rw, wob_ref[...], preferred_element_type=F32)
    h1 = x_ref[0] + g1_ref[0] * mix
    n2 = (_rms(h1, n2g_ref[...] * (1.0 + sc2_ref[0])) + sh2_ref[0]).astype(BF16)
    d_ff = w1_ref.shape[1]
    slices = [slice(s * ff_tile, (s + 1) * ff_tile) for s in range(d_ff // ff_tile)]

    def up(sl):
        return jnp.dot(n2, w1_ref[:, sl], preferred_element_type=F32) + b1_ref[:, sl]

    acc = jnp.zeros(h1.shape, F32)
    hid = up(slices[0])
    for s, sl in enumerate(slices):
        nxt = up(slices[s + 1]) if s + 1 < len(slices) else None
        act = jnp.square(jnp.maximum(hid, 0.0)).astype(BF16)
        acc = acc + jnp.dot(act, w2_ref[sl, :], preferred_element_type=F32)
        hid = nxt
    h2 = h1 + g2_ref[0] * (acc + b2_ref[...])
    o_ref[0] = _rms(h2, fg_ref[...])


def _mlp(x, ret, yf, yb, bonus, gate, ln_w, ln_b, red, bc2, g1, sh2, sc2, g2, n2g, fg,
         woa, wob, w1, b1, w2, b2, tm):
    bsz, t, d = x.shape
    const = lambda b, i: (0, 0)
    single = pl.Buffered(1)
    mod = pl.BlockSpec((1, 1, d), lambda b, i: (b, 0, 0))
    cw = lambda a: pl.BlockSpec(a.shape, const, pipeline_mode=single)
    tok = lambda w: pl.BlockSpec((1, tm, w), lambda b, i: (b, i, 0))
    return pl.pallas_call(
        functools.partial(_mlp_kernel, ff_tile=1024),
        grid=(bsz, t // tm),
        in_specs=[tok(d), tok(RET_WIDTH), tok(RWKV_WIDTH), tok(RWKV_WIDTH), tok(RWKV_WIDTH),
                  tok(RWKV_WIDTH), cw(ln_w), cw(ln_b), cw(red), cw(bc2),
                  mod, mod, mod, mod, cw(n2g), cw(fg),
                  cw(woa), cw(wob), cw(w1), cw(b1), cw(w2), cw(b2)],
        out_specs=tok(d),
        out_shape=jax.ShapeDtypeStruct((bsz, t, d), F32),
        compiler_params=_params("parallel", "parallel"),
        name="outproj_mlp",
    )(x, ret, yf, yb, bonus, gate, ln_w, ln_b, red, bc2, g1, sh2, sc2, g2, n2g, fg,
      woa, wob, w1, b1, w2, b2)


def _rope_tables(t):
    half = RET_HEAD_DIM // 2
    inv = jnp.power(ROPE_BASE, -jnp.arange(0, half, 2, dtype=F32) / half)
    pos = jnp.arange(t)
    ang_r = (pos // GRID_W).astype(F32)[:, None] * inv[None, :]
    ang_c = (pos % GRID_W).astype(F32)[:, None] * inv[None, :]
    cos_t = jnp.concatenate([jnp.cos(ang_r)] * 2 + [jnp.cos(ang_c)] * 2, axis=-1)
    sin_t = jnp.concatenate([-jnp.sin(ang_r), jnp.sin(ang_r), -jnp.sin(ang_c), jnp.sin(ang_c)], axis=-1)
    return cos_t, sin_t


def kernel(x, c, ctx, c_ctx, w_ada, b_ada, norm1_g, norm2_g, w_in, ret_log_decay, rwkv_shift_mu,
           rwkv_w0, rwkv_w_up, rwkv_a0, rwkv_a_up, rwkv_g_up, rwkv_k_k, rwkv_k_a, rwkv_r_k,
           rwkv_ln_w, rwkv_ln_b, w_out, w_ff1, b_ff1, w_ff2, b_ff2, final_g):
    bsz, t, d = x.shape
    assert w_ada.shape[0] == 1, "single trunk layer"
    assert bsz % RWKV_BATCH_BLOCK == 0
    l = 0

    rows = -(-(bsz + 1) // SUBLANES) * SUBLANES
    cc = jnp.zeros((rows, d), F32).at[:bsz].set(c).at[bsz].set(c_ctx)
    mods = _ada(cc, w_ada[l], b_ada[l][None, :])
    sh1, sc1, g1, sh2, sc2, g2 = [m[:bsz, None, :] for m in jnp.split(mods, 6, axis=-1)]
    csh1, csc1 = [jnp.broadcast_to(m[bsz][None, None, :], (bsz, 1, d))
                  for m in jnp.split(mods, 6, axis=-1)[:2]]

    w_in_b = w_in[l].astype(BF16)
    w_ret, w_rw = w_in_b[:, :RET_COLS], w_in_b[:, RET_COLS:]
    n1g = norm1_g[l][None, :]
    mu = rwkv_shift_mu[l]
    p_ret, p_rw = _inproj(x, sh1, sc1, n1g, w_ret, w_rw, mu, tm=TOKEN_TILE)
    pc_ret, pc_rw = _inproj(ctx, csh1, csc1, n1g, w_ret, w_rw, mu, tm=ctx.shape[1])

    cos_t, sin_t = _rope_tables(t)
    ld = jnp.broadcast_to(ret_log_decay[l][:, :, None, None], (2, RET_HEADS, SUBLANES, LANES))
    ret_out = _retention(p_ret, pc_ret, cos_t, sin_t, ld)

    head_id = jnp.arange(RWKV_WIDTH) // RWKV_HEAD_DIM
    red = (head_id[:, None] == jnp.arange(LANES)[None, :]).astype(BF16)
    bc2 = jnp.concatenate([red.T, red.T], axis=0)
    rw_params = (rwkv_w0[l], rwkv_w_up[l].astype(BF16), rwkv_a0[l],
                 rwkv_a_up[l].astype(BF16), rwkv_g_up[l].astype(BF16),
                 rwkv_k_k[l][None, :], rwkv_k_a[l][None, :], rwkv_r_k[l][None, :], red, bc2)
    zeros = jnp.zeros((bsz, 2, RWKV_STATE_PAIRS, LANES, LANES), F32)
    s_ctx = _rwkv_scan(pc_rw, zeros, *rw_params)[4]
    yf, yb, bonus, gate, _ = _rwkv_scan(p_rw, s_ctx, *rw_params)

    wo = w_out[l].astype(BF16)
    return _mlp(x, ret_out, yf, yb, bonus, gate, rwkv_ln_w[l][None, :], rwkv_ln_b[l][None, :], red, bc2,
                g1, sh2, sc2, g2, norm2_g[l][None, :], final_g[None, :],
                wo[:RET_WIDTH], wo[RET_WIDTH:], w_ff1[l].astype(BF16), b_ff1[l][None, :],
                w_ff2[l].astype(BF16), b_ff2[l][None, :], tm=TOKEN_TILE)
```

```python
import functools
import math

import jax
import jax.numpy as jnp
from jax import lax
from jax.experimental import pallas as pl
from jax.experimental.pallas import tpu as pltpu

F32 = jnp.float32
BF16 = jnp.bfloat16

GRID_W = 64
RET_HEADS = 4
RET_HEAD_DIM = 128
RET_WIDTH = RET_HEADS * RET_HEAD_DIM
RET_CHUNK = 128
RWKV_HEAD_DIM = 64
RWKV_WIDTH = 512
RWKV_GROUP_HEADS = 2
RWKV_GROUP_LANES = RWKV_GROUP_HEADS * RWKV_HEAD_DIM
RWKV_GROUPS = RWKV_WIDTH // RWKV_GROUP_LANES
RWKV_STATE_PAIRS = RWKV_WIDTH // 128
RWKV_CHUNK = 64
RWKV_BATCH_BLOCK = 8
RWKV_CHAIN_ROWS = 2
PREP_EVERY_CHAIN_STAGES = 3
INV_BASE_BLOCK = 16
DECAY_LORA = 64
AAA_LORA = 64
GATE_LORA = 128
SHIFT_COLS = 3 * RWKV_WIDTH + DECAY_LORA + AAA_LORA + GATE_LORA
RET_COLS = 4 * RET_WIDTH
ROPE_BASE = 10000.0
NORM_EPS = 1e-6
GN_EPS = 64e-5
W_DECAY_SCALE = math.exp(-0.5)
LANES = 128
SUBLANES = 8
TOKEN_TILE = 512
MERGE_ROW_BLOCK = 128
VMEM_LIMIT = 56 * 1024 * 1024


def _params(*sem):
    return pltpu.CompilerParams(dimension_semantics=sem, vmem_limit_bytes=VMEM_LIMIT)


def _mm(a, b):
    return jnp.dot(a.astype(BF16), b.astype(BF16), preferred_element_type=F32)


def _mm_nt(a, b):
    return lax.dot_general(a.astype(BF16), b.astype(BF16), (((1,), (1,)), ((), ())),
                           preferred_element_type=F32)


def _mm_tn(a, b):
    return lax.dot_general(a.astype(BF16), b.astype(BF16), (((0,), (0,)), ((), ())),
                           preferred_element_type=F32)


def _split3(x):
    hi = x.astype(BF16)
    r1 = x - hi.astype(F32)
    mid = r1.astype(BF16)
    lo = (r1 - mid.astype(F32)).astype(BF16)
    return hi, mid, lo


def _mm_exact_rhs(a_bf16, x):
    return jnp.dot(jnp.concatenate([a_bf16] * 3, axis=1), jnp.concatenate(_split3(x), axis=0),
                   preferred_element_type=F32)


def _split2(x):
    hi = x.astype(BF16)
    return hi, (x - hi.astype(F32)).astype(BF16)


def _rms(x, g):
    return x * lax.rsqrt(jnp.mean(x * x, axis=-1, keepdims=True) + NORM_EPS) * g


def _ada_kernel(c_ref, w_ref, b_ref, o_ref):
    s = jax.nn.silu(c_ref[...])
    o_ref[...] = jnp.dot(s, w_ref[...], preferred_element_type=F32,
                         precision=lax.Precision.HIGHEST) + b_ref[...]


def _ada(cc, w, b):
    rows, d = cc.shape
    n = w.shape[1]
    tn = n // 4
    return pl.pallas_call(
        _ada_kernel,
        grid=(n // tn,),
        in_specs=[pl.BlockSpec((rows, d), lambda j: (0, 0)),
                  pl.BlockSpec((d, tn), lambda j: (0, j)),
                  pl.BlockSpec((1, tn), lambda j: (0, j))],
        out_specs=pl.BlockSpec((rows, tn), lambda j: (0, j)),
        out_shape=jax.ShapeDtypeStruct((rows, n), F32),
        compiler_params=_params("arbitrary"),
        name="adaln",
    )(cc, w, b)


def _inproj_kernel(x_ref, xp_ref, xn_ref, sh_ref, sc_ref, g_ref, wr_ref, ww_ref, mu_ref,
                   pret_ref, prw_ref):
    i = pl.program_id(1)
    n = pl.num_programs(1)
    tm = x_ref.shape[1]
    halo = SUBLANES

    gain = g_ref[...] * (1.0 + sc_ref[0])

    def modulated(xv):
        return xv * lax.rsqrt(jnp.mean(xv * xv, axis=-1, keepdims=True) + NORM_EPS) * gain + sh_ref[0]

    h = modulated(x_ref[0])
    h_ext = jnp.concatenate([modulated(xp_ref[0]), h, modulated(xn_ref[0])], axis=0).astype(BF16)
    p_ext = jnp.dot(h_ext, ww_ref[...], preferred_element_type=F32)
    rid = lax.broadcasted_iota(jnp.int32, (tm + 2 * halo, 1), 0)
    inside = jnp.logical_and(jnp.logical_or(rid >= halo, i > 0),
                             jnp.logical_or(rid < tm + halo, i < n - 1))
    ext = jnp.where(inside, p_ext, 0.0)
    pret_ref[0] = jnp.dot(h.astype(BF16), wr_ref[...], preferred_element_type=F32)
    rows = tm + 2 * halo
    p = ext[halo:halo + tm]
    prev = pltpu.roll(ext, 1, axis=0)[halo:halo + tm]
    nxt = pltpu.roll(ext, rows - 1, axis=0)[halo:halo + tm]
    mu_prev, mu_next = mu_ref[0:1, :], mu_ref[1:2, :]
    prw_ref[0] = (1.0 - mu_prev - mu_next) * p + mu_prev * prev + mu_next * nxt


def _inproj(x, shift, scale, g, w_ret, w_rw, mu, tm):
    bsz, t, d = x.shape
    const = lambda b, i: (0, 0)
    blk8 = tm // SUBLANES
    last8 = t // SUBLANES - 1
    return pl.pallas_call(
        _inproj_kernel,
        grid=(bsz, t // tm),
        in_specs=[pl.BlockSpec((1, tm, d), lambda b, i: (b, i, 0)),
                  pl.BlockSpec((1, SUBLANES, d), lambda b, i: (b, jnp.maximum(i * blk8 - 1, 0), 0)),
                  pl.BlockSpec((1, SUBLANES, d), lambda b, i: (b, jnp.minimum((i + 1) * blk8, last8), 0)),
                  pl.BlockSpec((1, 1, d), lambda b, i: (b, 0, 0)),
                  pl.BlockSpec((1, 1, d), lambda b, i: (b, 0, 0)),
                  pl.BlockSpec((1, d), const),
                  pl.BlockSpec(w_ret.shape, const, pipeline_mode=pl.Buffered(1)),
                  pl.BlockSpec(w_rw.shape, const, pipeline_mode=pl.Buffered(1)),
                  pl.BlockSpec(mu.shape, const)],
        out_specs=[pl.BlockSpec((1, tm, RET_COLS), lambda b, i: (b, i, 0)),
                   pl.BlockSpec((1, tm, SHIFT_COLS), lambda b, i: (b, i, 0))],
        out_shape=[jax.ShapeDtypeStruct((bsz, t, RET_COLS), F32),
                   jax.ShapeDtypeStruct((bsz, t, SHIFT_COLS), F32)],
        compiler_params=_params("parallel", "parallel"),
        name="inproj",
    )(x, x, x, shift, scale, g, w_ret, w_rw, mu)


def _ret_kernel(q_ref, k_ref, v_ref, g_ref, kc_ref, vc_ref, cos_ref, sin_ref, ld_ref, o_ref):
    c = RET_CHUNK
    hd = RET_HEAD_DIM
    t = q_ref.shape[1]
    tc = kc_ref.shape[1]
    n = t // c
    k_scale = hd ** -0.5
    lgf = -jnp.exp(ld_ref[0, 0][0:1, :])
    lgb = -jnp.exp(ld_ref[1, 0][0:1, :])

    jc = lax.broadcasted_iota(jnp.int32, (tc, 1), 0).astype(F32)
    kc = kc_ref[0] * k_scale
    vc = vc_ref[0]
    s_f = _mm_tn(kc * jnp.exp(lgf * (tc - 1.0 - jc)), vc)
    s_b = _mm_tn(kc * jnp.exp(lgb * jc), vc)

    def rope(x, cs, sn):
        return x * cs + pltpu.roll(x, hd // 2, axis=1) * sn

    ii = lax.broadcasted_iota(jnp.int32, (c, 1), 0).astype(F32)
    dist = (lax.broadcasted_iota(jnp.int32, (c, c), 0)
            - lax.broadcasted_iota(jnp.int32, (c, c), 1)).astype(F32)
    decay = jnp.where(dist >= 0, jnp.exp(lgf * jnp.maximum(dist, 0.0)),
                      jnp.exp(lgb * jnp.maximum(-dist, 0.0)))
    qdec_f = jnp.exp(lgf * (ii + 1.0))
    kdec_f = jnp.exp(lgf * (c - 1.0 - ii))
    qdec_b = jnp.exp(lgb * (c - ii))
    kdec_b = jnp.exp(lgb * ii)
    cdec_f = jnp.exp(lgf * c)
    cdec_b = jnp.exp(lgb * c)

    chunks = [slice(ci * c, (ci + 1) * c) for ci in range(n)]
    qs = [rope(q_ref[0, sl, :], cos_ref[sl, :], sin_ref[sl, :]) for sl in chunks]
    ks = [rope(k_ref[0, sl, :] * k_scale, cos_ref[sl, :], sin_ref[sl, :]) for sl in chunks]
    scores = [_mm_nt(qc, kx) for qc, kx in zip(qs, ks)]
    kvs = [_mm_tn(jnp.concatenate([kx * kdec_f, kx * kdec_b], axis=1), v_ref[0, sl, :])
           for kx, sl in zip(ks, chunks)]

    sfs, sbs = [None] * n, [None] * n
    for ci in range(n):
        sfs[ci] = s_f
        s_f = s_f * cdec_f + kvs[ci][:hd]
    for ci in range(n - 1, -1, -1):
        sbs[ci] = s_b
        s_b = s_b * cdec_b + kvs[ci][hd:]

    outs = [_mm(jnp.concatenate([sc * decay, qc * qdec_f, qc * qdec_b], axis=1),
                jnp.concatenate([v_ref[0, sl, :].astype(BF16), sf.astype(BF16), sb.astype(BF16)], axis=0))
            for sc, qc, sl, sf, sb in zip(scores, qs, chunks, sfs, sbs)]
    for o, sl in zip(outs, chunks):
        o = o * lax.rsqrt(jnp.mean(o * o, axis=-1, keepdims=True) + NORM_EPS)
        o_ref[0, sl, :] = (o * jax.nn.silu(g_ref[0, sl, :])).astype(o_ref.dtype)


def _retention(p_ret, pc_ret, cos_t, sin_t, ld):
    bsz, t, _ = p_ret.shape
    tc = pc_ret.shape[1]
    hd = RET_HEAD_DIM
    col = lambda j: pl.BlockSpec((1, t, hd), lambda b, h: (b, 0, h + RET_HEADS * j))
    colc = lambda j: pl.BlockSpec((1, tc, hd), lambda b, h: (b, 0, h + RET_HEADS * j))
    return pl.pallas_call(
        _ret_kernel,
        grid=(bsz, RET_HEADS),
        in_specs=[col(0), col(1), col(2), col(3), colc(1), colc(2),
                  pl.BlockSpec((t, hd), lambda b, h: (0, 0)),
                  pl.BlockSpec((t, hd), lambda b, h: (0, 0)),
                  pl.BlockSpec((2, 1, SUBLANES, LANES), lambda b, h: (0, h, 0, 0))],
        out_specs=pl.BlockSpec((1, t, hd), lambda b, h: (b, 0, h)),
        out_shape=jax.ShapeDtypeStruct((bsz, t, RET_WIDTH), BF16),
        compiler_params=_params("parallel", "parallel"),
        name="retention",
    )(p_ret, p_ret, p_ret, p_ret, pc_ret, pc_ret, cos_t, sin_t, ld)


def _group_expand(y, head_of_lane):
    zero = jnp.zeros_like(y)
    return jnp.concatenate([jnp.where(head_of_lane == h, y, zero) for h in range(RWKV_GROUP_HEADS)],
                           axis=0)


def _mm_group(a, y, head_of_lane):
    return _mm(a, _group_expand(y.astype(BF16), head_of_lane))


def _rwkv_scan_kernel(pf_ref, pb_ref, w0_ref, wup_ref, a0_ref, aup_ref, gup_ref, kk_ref, ka_ref, rk_ref,
                      red_ref, bc_ref, s0_ref, yf_ref, yb_ref, bonus_ref, gate_ref, fin_ref, st_s):
    c = RWKV_CHUNK
    w_ = RWKV_WIDTH
    hd = RWKV_HEAD_DIM
    nb = pf_ref.shape[0]
    j = pl.program_id(1)
    n = pl.num_programs(1)

    per_group = RWKV_GROUP_LANES // LANES

    def pair_window(ref, p):
        g, o = divmod(p, per_group)
        return ref.at[:, :, g, o * LANES:(o + 1) * LANES, o * LANES:(o + 1) * LANES]

    @pl.when(j == 0)
    def _():
        st_s[...] = jnp.zeros(st_s.shape, F32)
        for p in range(s0_ref.shape[2]):
            pair_window(st_s, p)[...] = s0_ref[:, :, p]

    row = lax.broadcasted_iota(jnp.int32, (c, c), 0)
    colm = lax.broadcasted_iota(jnp.int32, (c, c), 1)
    gl = RWKV_GROUP_LANES
    lane = lax.broadcasted_iota(jnp.int32, (c, gl), 1)
    trow = lax.broadcasted_iota(jnp.int32, (c, gl), 0)
    head_of = lane // hd
    jj = lane % hd
    r2 = lax.broadcasted_iota(jnp.int32, (gl, gl), 0)
    c2 = lax.broadcasted_iota(jnp.int32, (gl, gl), 1)
    blockdiag = (r2 // hd) == (c2 // hd)
    eye_cat = jnp.where(jj == trow, 1.0, 0.0)
    red = red_ref[...]
    bc2 = bc_ref[...]

    def same_block(size):
        return (trow // size) == (jj // size)

    def stacked(fn, xs):
        out = fn(jnp.concatenate(xs, axis=0))
        return [out[i * c:(i + 1) * c] for i in range(len(xs))]

    def head_sums(xs):
        return stacked(lambda x: jnp.dot(x.astype(BF16), red, preferred_element_type=F32), xs)

    def head_bcast(ss):
        return stacked(lambda s: jnp.dot(jnp.concatenate(_split2(s), axis=1), bc2,
                                         preferred_element_type=F32), ss)

    def prep_stages(rows, out):
        items = [dict(bi=bi, d=d, fwd=d == 0) for bi in rows for d in range(2)]
        for it in items:
            ps = (pf_ref if it["fwd"] else pb_ref)[it["bi"]]
            k = ps[:, w_:2 * w_]
            it.update(ps=ps, k=k, kk0=k * kk_ref[...])
        for it, s in zip(items, head_sums([it["kk0"] * it["kk0"] for it in items])):
            it["nrm"] = s
        for d in range(2):
            sel = [it for it in items if it["d"] == d]
            wls = [jnp.tanh(it["ps"][:, 3 * w_:3 * w_ + DECAY_LORA]) for it in sel]
            als = [it["ps"][:, 3 * w_ + DECAY_LORA:3 * w_ + DECAY_LORA + AAA_LORA] for it in sel]
            for it, wp, ap in zip(sel, stacked(lambda x: _mm(x, wup_ref[d]), wls),
                                  stacked(lambda x: _mm(x, aup_ref[d]), als)):
                it.update(w_pre=wp, a_pre=ap)
        yield
        for it, nb_ in zip(items, head_bcast([lax.rsqrt(it["nrm"] + 1e-12) for it in items])):
            it["inv_norm"] = nb_
        for it in items:
            d = it["d"]
            logw = -W_DECAY_SCALE * jax.nn.sigmoid(w0_ref[d:d + 1, :] + it["w_pre"])
            tri = ((colm <= row) if it["fwd"] else (colm >= row)).astype(BF16)
            it.update(logw=logw, big_l=_mm_exact_rhs(tri, logw),
                      a=jax.nn.sigmoid(a0_ref[d:d + 1, :] + it["a_pre"]))
        yield
        for it in items:
            bi, fwd, ps, k, a, logw, big_l = (it[x] for x in ("bi", "fwd", "ps", "k", "a", "logw", "big_l"))
            r = ps[:, 0:w_]
            v = ps[:, 2 * w_:3 * w_]
            kk = it["kk0"] * it["inv_norm"]
            kt = k * (1.0 + (a - 1.0) * ka_ref[...])
            b = kk * a
            it.update(v=v, rkt=r * kt * rk_ref[...])
            ltot = big_l[c - 1:c, :] if fwd else big_l[0:1, :]
            e_lx = jnp.exp(big_l - logw)
            e_nl = jnp.exp(-big_l)
            e_tot = jnp.exp(ltot)
            e_rest = e_tot * e_nl
            strict = (jj < trow) if fwd else (jj > trow)
            out.append(dict(bi=bi, d=it["d"], v=v, at=-kk * e_lx, bt=b * e_nl, ktl=kt * e_nl,
                            rt=r * (jnp.exp(big_l) if fwd else e_lx), bh=b * e_rest, kh=kt * e_rest,
                            e_tot=e_tot, strict=strict,
                            omask=(jj <= trow) if fwd else strict))
        fw = [it for it in items if it["fwd"]]
        gls = [jax.nn.sigmoid(it["ps"][:, 3 * w_ + DECAY_LORA + AAA_LORA:]) for it in fw]
        for it, g_ in zip(fw, stacked(lambda x: _mm(x, gup_ref[...]), gls)):
            gate_ref[it["bi"]] = g_
        bsums = head_sums([it["rkt"] for it in fw])
        yield
        for it, bb in zip(fw, head_bcast(bsums)):
            bonus_ref[it["bi"]] = bb * it["v"]

    def chain_stages(ops):
        chains = []
        for po in ops:
            bi, d = po["bi"], po["d"]
            for pr in range(RWKV_GROUPS):
                sl = slice(pr * gl, (pr + 1) * gl)
                lhs = jnp.concatenate([po["at"][:, sl], po["rt"][:, sl]], axis=0)
                m = _mm_nt(lhs, jnp.concatenate([_group_expand(po["bt"][:, sl].astype(BF16), head_of),
                                                 _group_expand(po["ktl"][:, sl].astype(BF16), head_of)], axis=0))
                s_in = st_s[bi, d, pr]
                ch = dict(bi=bi, d=d, pr=pr, sl=sl, v=po["v"][:, sl], bh=po["bh"][:, sl],
                          kh=po["kh"][:, sl], e_tot=po["e_tot"][:, sl], s_in=s_in, sx=_mm_nt(lhs, s_in),
                          a_ab=jnp.where(po["strict"], m[:c, :gl], 0.0),
                          a_ak=jnp.where(po["strict"], m[:c, gl:], 0.0),
                          m_rb=jnp.where(po["omask"], m[c:, :gl], 0.0),
                          m_rk=jnp.where(po["omask"], m[c:, gl:], 0.0))
                chains.append(ch)
        yield

        a_diag = [jnp.where(same_block(INV_BASE_BLOCK), ch["a_ab"], 0.0) for ch in chains]
        tinv = [eye_cat + a for a in a_diag]
        pw = [_mm_group(p, p, head_of) for p in a_diag]
        yield
        span = 4
        while span < INV_BASE_BLOCK:
            both = [_mm_group(jnp.concatenate([t, p], axis=0), p, head_of) for t, p in zip(tinv, pw)]
            tinv = [t + r[:c] for t, r in zip(tinv, both)]
            pw = [r[c:] for r in both]
            yield
            span *= 2
        tinv = [t + _mm_group(t, p, head_of) for t, p in zip(tinv, pw)]
        yield
        size = INV_BASE_BLOCK
        while size < c:
            off = jnp.logical_and(same_block(2 * size), jnp.logical_not(same_block(size)))
            ta = [_mm_group(t, jnp.where(off, ch["a_ab"], 0.0), head_of) for t, ch in zip(tinv, chains)]
            yield
            tinv = [t + _mm_group(x, t, head_of) for t, x in zip(tinv, ta)]
            yield
            size *= 2

        avs = [_mm(jnp.concatenate([ch["a_ak"], ch["m_rk"]], axis=0),
                   _group_expand(ch["v"].astype(BF16), head_of)) for ch in chains]
        yield
        us = [_mm_group(t_m, ch["sx"][:c] + av[:c], head_of) for ch, t_m, av in zip(chains, tinv, avs)]
        yield
        for ch, av, u in zip(chains, avs, us):
            y_ref = yf_ref if ch["d"] == 0 else yb_ref
            y_ref[ch["bi"], :, ch["sl"]] = ch["sx"][c:] + av[c:] + _mm_group(ch["m_rb"], u, head_of)
        upds = [_mm_tn(jnp.concatenate([u, ch["v"]], axis=0), jnp.concatenate([ch["bh"], ch["kh"]], axis=0))
                for ch, u in zip(chains, us)]
        yield
        for ch, upd in zip(chains, upds):
            st_s[ch["bi"], ch["d"], ch["pr"]] = ch["s_in"] * ch["e_tot"] + jnp.where(blockdiag, upd, 0.0)

    groups = [range(g, min(g + RWKV_CHAIN_ROWS, nb)) for g in range(0, nb, RWKV_CHAIN_ROWS)]
    ops = []
    for _ in prep_stages(groups[0], ops):
        pass
    for gi in range(len(groups)):
        nxt = []
        prep = prep_stages(groups[gi + 1], nxt) if gi + 1 < len(groups) else iter(())
        for lvl, _ in enumerate(chain_stages(ops)):
            if lvl % PREP_EVERY_CHAIN_STAGES == 0:
                next(prep, None)
        for _ in prep:
            pass
        ops = nxt

    @pl.when(j == n - 1)
    def _():
        for p in range(fin_ref.shape[2]):
            fin_ref[:, :, p] = pair_window(st_s, p)[...]


def _rwkv_scan(p_rw, s0, w0, wup, a0, aup, gup, k_k, k_a, r_k, red, bc2):
    bsz, t, cols = p_rw.shape
    c = RWKV_CHUNK
    nb = RWKV_BATCH_BLOCK
    n = t // c
    const2 = lambda b, i: (0, 0)
    const3 = lambda b, i: (0, 0, 0)
    state_blk = (nb, 2, RWKV_STATE_PAIRS, LANES, LANES)
    state_map = lambda b, i: (b, 0, 0, 0, 0)
    state_shape = jax.ShapeDtypeStruct((bsz, 2, RWKV_STATE_PAIRS, LANES, LANES), F32)
    tok_shape = jax.ShapeDtypeStruct((bsz, t, RWKV_WIDTH), F32)
    tok_f = pl.BlockSpec((nb, c, RWKV_WIDTH), lambda b, i: (b, i, 0))
    tok_b = pl.BlockSpec((nb, c, RWKV_WIDTH), lambda b, i: (b, n - 1 - i, 0))
    return pl.pallas_call(
        _rwkv_scan_kernel,
        grid=(bsz // nb, n),
        in_specs=[
            pl.BlockSpec((nb, c, cols), lambda b, i: (b, i, 0)),
            pl.BlockSpec((nb, c, cols), lambda b, i: (b, n - 1 - i, 0)),
            pl.BlockSpec(w0.shape, const2),
            pl.BlockSpec(wup.shape, const3), pl.BlockSpec(a0.shape, const2),
            pl.BlockSpec(aup.shape, const3), pl.BlockSpec(gup.shape, const2),
            pl.BlockSpec(k_k.shape, const2), pl.BlockSpec(k_a.shape, const2),
            pl.BlockSpec(r_k.shape, const2), pl.BlockSpec(red.shape, const2),
            pl.BlockSpec(bc2.shape, const2),
            pl.BlockSpec(state_blk, state_map, pipeline_mode=pl.Buffered(1))],
        out_specs=[tok_f, tok_b, tok_f, tok_f, pl.BlockSpec(state_blk, state_map)],
        out_shape=[tok_shape] * 4 + [state_shape],
        scratch_shapes=[pltpu.VMEM((nb, 2, RWKV_GROUPS, RWKV_GROUP_LANES, RWKV_GROUP_LANES), F32)],
        compiler_params=_params("parallel", "arbitrary"),
        name="rwkv_scan",
    )(p_rw, p_rw, w0, wup, a0, aup, gup, k_k, k_a, r_k, red, bc2, s0)


def _mlp_kernel(x_ref, ret_ref, yf_ref, yb_ref, bonus_ref, gate_ref, lnw_ref, lnb_ref, red_ref, bc_ref,
                g1_ref, sh2_ref, sc2_ref, g2_ref, n2g_ref, fg_ref,
                woa_ref, wob_ref, w1_ref, b1_ref, w2_ref, b2_ref, o_ref, *, ff_tile):
    mix_ret = jnp.dot(ret_ref[0], woa_ref[...], preferred_element_type=F32)
    red = red_ref[...]
    bc2 = bc_ref[...]
    inv_n = 1.0 / RWKV_HEAD_DIM
    blk = MERGE_ROW_BLOCK
    rows = [slice(r, r + blk) for r in range(0, x_ref.shape[1], blk)]

    def head_means(xs, two_piece):
        if two_piece:
            ss = [jnp.dot(jnp.concatenate(_split2(x), axis=0), red, preferred_element_type=F32) for x in xs]
            ss = [s[:blk] + s[blk:] for s in ss]
        else:
            ss = [jnp.dot(x.astype(BF16), red, preferred_element_type=F32) for x in xs]
        return [jnp.dot(jnp.concatenate(_split2(s * inv_n), axis=1), bc2, preferred_element_type=F32)
                for s in ss]

    ys = [yf_ref[0, r, :] + yb_ref[0, r, :] for r in rows]
    ycs = [y - m for y, m in zip(ys, head_means(ys, two_piece=True))]
    variances = head_means([yc * yc for yc in ycs], two_piece=False)
    rw = jnp.concatenate(
        [((yc * lax.rsqrt(var + GN_EPS) * lnw_ref[...] + lnb_ref[...] + bonus_ref[0, r, :])
          * gate_ref[0, r, :]).astype(BF16) for yc, var, r in zip(ycs, variances, rows)], axis=0)
    mix = mix_ret + jnp.dot(rw, wob_ref[...], preferred_element_type=F32)
    h1 = x_ref[0] + g1_ref[0] * mix
    n2 = (_rms(h1, n2g_ref[...] * (1.0 + sc2_ref[0])) + sh2_ref[0]).astype(BF16)
    d_ff = w1_ref.shape[1]
    slices = [slice(s * ff_tile, (s + 1) * ff_tile) for s in range(d_ff // ff_tile)]

    def up(sl):
        return jnp.dot(n2, w1_ref[:, sl], preferred_element_type=F32) + b1_ref[:, sl]

    acc = jnp.zeros(h1.shape, F32)
    hid = up(slices[0])
    for s, sl in enumerate(slices):
        nxt = up(slices[s + 1]) if s + 1 < len(slices) else None
        act = jnp.square(jnp.maximum(hid, 0.0)).astype(BF16)
        acc = acc + jnp.dot(act, w2_ref[sl, :], preferred_element_type=F32)
        hid = nxt
    h2 = h1 + g2_ref[0] * (acc + b2_ref[...])
    o_ref[0] = _rms(h2, fg_ref[...])


def _mlp(x, ret, yf, yb, bonus, gate, ln_w, ln_b, red, bc2, g1, sh2, sc2, g2, n2g, fg,
         woa, wob, w1, b1, w2, b2, tm):
    bsz, t, d = x.shape
    const = lambda b, i: (0, 0)
    single = pl.Buffered(1)
    mod = pl.BlockSpec((1, 1, d), lambda b, i: (b, 0, 0))
    cw = lambda a: pl.BlockSpec(a.shape, const, pipeline_mode=single)
    tok = lambda w: pl.BlockSpec((1, tm, w), lambda b, i: (b, i, 0))
    return pl.pallas_call(
        functools.partial(_mlp_kernel, ff_tile=1024),
        grid=(bsz, t // tm),
        in_specs=[tok(d), tok(RET_WIDTH), tok(RWKV_WIDTH), tok(RWKV_WIDTH), tok(RWKV_WIDTH),
                  tok(RWKV_WIDTH), cw(ln_w), cw(ln_b), cw(red), cw(bc2),
                  mod, mod, mod, mod, cw(n2g), cw(fg),
                  cw(woa), cw(wob), cw(w1), cw(b1), cw(w2), cw(b2)],
        out_specs=tok(d),
        out_shape=jax.ShapeDtypeStruct((bsz, t, d), F32),
        compiler_params=_params("parallel", "parallel"),
        name="outproj_mlp",
    )(x, ret, yf, yb, bonus, gate, ln_w, ln_b, red, bc2, g1, sh2, sc2, g2, n2g, fg,
      woa, wob, w1, b1, w2, b2)


def _rope_tables(t):
    half = RET_HEAD_DIM // 2
    inv = jnp.power(ROPE_BASE, -jnp.arange(0, half, 2, dtype=F32) / half)
    pos = jnp.arange(t)
    ang_r = (pos // GRID_W).astype(F32)[:, None] * inv[None, :]
    ang_c = (pos % GRID_W).astype(F32)[:, None] * inv[None, :]
    cos_t = jnp.concatenate([jnp.cos(ang_r), jnp.cos(ang_c)] * 2, axis=-1)
    sin_t = jnp.concatenate([-jnp.sin(ang_r), -jnp.sin(ang_c), jnp.sin(ang_r), jnp.sin(ang_c)], axis=-1)
    return cos_t, sin_t


def _rope_column_order():
    quarter = RET_HEAD_DIM // 4
    head = jnp.concatenate([jnp.arange(s * quarter, (s + 1) * quarter) for s in (0, 2, 1, 3)])
    qk = jnp.concatenate([h * RET_HEAD_DIM + head for h in range(2 * RET_HEADS)])
    return jnp.concatenate([qk, jnp.arange(2 * RET_WIDTH, RET_COLS)])


def kernel(x, c, ctx, c_ctx, w_ada, b_ada, norm1_g, norm2_g, w_in, ret_log_decay, rwkv_shift_mu,
           rwkv_w0, rwkv_w_up, rwkv_a0, rwkv_a_up, rwkv_g_up, rwkv_k_k, rwkv_k_a, rwkv_r_k,
           rwkv_ln_w, rwkv_ln_b, w_out, w_ff1, b_ff1, w_ff2, b_ff2, final_g):
    bsz, t, d = x.shape
    assert w_ada.shape[0] == 1, "single trunk layer"
    assert bsz % RWKV_BATCH_BLOCK == 0
    l = 0

    rows = -(-(bsz + 1) // SUBLANES) * SUBLANES
    cc = jnp.zeros((rows, d), F32).at[:bsz].set(c).at[bsz].set(c_ctx)
    mods = _ada(cc, w_ada[l], b_ada[l][None, :])
    sh1, sc1, g1, sh2, sc2, g2 = [m[:bsz, None, :] for m in jnp.split(mods, 6, axis=-1)]
    csh1, csc1 = [jnp.broadcast_to(m[bsz][None, None, :], (bsz, 1, d))
                  for m in jnp.split(mods, 6, axis=-1)[:2]]

    w_in_b = w_in[l].astype(BF16)
    w_ret, w_rw = w_in_b[:, :RET_COLS][:, _rope_column_order()], w_in_b[:, RET_COLS:]
    n1g = norm1_g[l][None, :]
    mu = rwkv_shift_mu[l]
    p_ret, p_rw = _inproj(x, sh1, sc1, n1g, w_ret, w_rw, mu, tm=TOKEN_TILE)
    pc_ret, pc_rw = _inproj(ctx, csh1, csc1, n1g, w_ret, w_rw, mu, tm=ctx.shape[1])

    cos_t, sin_t = _rope_tables(t)
    ld = jnp.broadcast_to(ret_log_decay[l][:, :, None, None], (2, RET_HEADS, SUBLANES, LANES))
    ret_out = _retention(p_ret, pc_ret, cos_t, sin_t, ld)

    head_id = jnp.arange(RWKV_WIDTH) // RWKV_HEAD_DIM
    red = (head_id[:, None] == jnp.arange(LANES)[None, :]).astype(BF16)
    bc2 = jnp.concatenate([red.T, red.T], axis=0)
    rw_params = (rwkv_w0[l], rwkv_w_up[l].astype(BF16), rwkv_a0[l],
                 rwkv_a_up[l].astype(BF16), rwkv_g_up[l].astype(BF16),
                 rwkv_k_k[l][None, :], rwkv_k_a[l][None, :], rwkv_r_k[l][None, :], red, bc2)
    zeros = jnp.zeros((bsz, 2, RWKV_STATE_PAIRS, LANES, LANES), F32)
    s_ctx = _rwkv_scan(pc_rw, zeros, *rw_params)[4]
    yf, yb, bonus, gate, _ = _rwkv_scan(p_rw, s_ctx, *rw_params)

    wo = w_out[l].astype(BF16)
    return _mlp(x, ret_out, yf, yb, bonus, gate, rwkv_ln_w[l][None, :], rwkv_ln_b[l][None, :], red, bc2,
                g1, sh2, sc2, g2, norm2_g[l][None, :], final_g[None, :],
                wo[:RET_WIDTH], wo[RET_WIDTH:], w_ff1[l].astype(BF16), b_ff1[l][None, :],
                w_ff2[l].astype(BF16), b_ff2[l][None, :], tm=TOKEN_TILE)
```

```python
import functools
import math

import jax
import jax.numpy as jnp
from jax import lax
from jax.experimental import pallas as pl
from jax.experimental.pallas import tpu as pltpu

F32 = jnp.float32
BF16 = jnp.bfloat16

GRID_W = 64
RET_HEADS = 4
RET_HEAD_DIM = 128
RET_WIDTH = RET_HEADS * RET_HEAD_DIM
RET_CHUNK = 128
RWKV_HEAD_DIM = 64
RWKV_WIDTH = 512
RWKV_GROUP_HEADS = 2
RWKV_GROUP_LANES = RWKV_GROUP_HEADS * RWKV_HEAD_DIM
RWKV_GROUPS = RWKV_WIDTH // RWKV_GROUP_LANES
RWKV_STATE_PAIRS = RWKV_WIDTH // 128
RWKV_CHUNK = 64
RWKV_BATCH_BLOCK = 8
RWKV_CHAIN_ROWS = 2
PREP_EVERY_CHAIN_STAGES = 3
INV_BASE_BLOCK = 16
DECAY_LORA = 64
AAA_LORA = 64
GATE_LORA = 128
SHIFT_COLS = 3 * RWKV_WIDTH + DECAY_LORA + AAA_LORA + GATE_LORA
RET_COLS = 4 * RET_WIDTH
ROPE_BASE = 10000.0
NORM_EPS = 1e-6
GN_EPS = 64e-5
W_DECAY_SCALE = math.exp(-0.5)
LANES = 128
SUBLANES = 8
TOKEN_TILE = 512
MERGE_ROW_BLOCK = 128
VMEM_LIMIT = 56 * 1024 * 1024


def _params(*sem):
    return pltpu.CompilerParams(dimension_semantics=sem, vmem_limit_bytes=VMEM_LIMIT)


def _mm(a, b):
    return jnp.dot(a.astype(BF16), b.astype(BF16), preferred_element_type=F32)


def _mm_nt(a, b):
    return lax.dot_general(a.astype(BF16), b.astype(BF16), (((1,), (1,)), ((), ())),
                           preferred_element_type=F32)


def _mm_tn(a, b):
    return lax.dot_general(a.astype(BF16), b.astype(BF16), (((0,), (0,)), ((), ())),
                           preferred_element_type=F32)


def _split3(x):
    hi = x.astype(BF16)
    r1 = x - hi.astype(F32)
    mid = r1.astype(BF16)
    lo = (r1 - mid.astype(F32)).astype(BF16)
    return hi, mid, lo


def _mm_exact_rhs(a_bf16, x):
    return jnp.dot(jnp.concatenate([a_bf16] * 3, axis=1), jnp.concatenate(_split3(x), axis=0),
                   preferred_element_type=F32)


def _split2(x):
    hi = x.astype(BF16)
    return hi, (x - hi.astype(F32)).astype(BF16)


def _rms(x, g):
    return x * lax.rsqrt(jnp.mean(x * x, axis=-1, keepdims=True) + NORM_EPS) * g


def _ada_kernel(c_ref, w_ref, b_ref, o_ref):
    s = jax.nn.silu(c_ref[...])
    o_ref[...] = jnp.dot(s, w_ref[...], preferred_element_type=F32,
                         precision=lax.Precision.HIGHEST) + b_ref[...]


def _ada(cc, w, b):
    rows, d = cc.shape
    n = w.shape[1]
    tn = n // 4
    return pl.pallas_call(
        _ada_kernel,
        grid=(n // tn,),
        in_specs=[pl.BlockSpec((rows, d), lambda j: (0, 0)),
                  pl.BlockSpec((d, tn), lambda j: (0, j)),
                  pl.BlockSpec((1, tn), lambda j: (0, j))],
        out_specs=pl.BlockSpec((rows, tn), lambda j: (0, j)),
        out_shape=jax.ShapeDtypeStruct((rows, n), F32),
        compiler_params=_params("arbitrary"),
        name="adaln",
    )(cc, w, b)


def _inproj_kernel(x_ref, xp_ref, xn_ref, sh_ref, sc_ref, g_ref, wr_ref, ww_ref, mu_ref,
                   pret_ref, prw_ref):
    i = pl.program_id(1)
    n = pl.num_programs(1)
    tm = x_ref.shape[1]
    halo = SUBLANES

    gain = g_ref[...] * (1.0 + sc_ref[0])

    def modulated(xv):
        return xv * lax.rsqrt(jnp.mean(xv * xv, axis=-1, keepdims=True) + NORM_EPS) * gain + sh_ref[0]

    h = modulated(x_ref[0])
    h_ext = jnp.concatenate([modulated(xp_ref[0]), h, modulated(xn_ref[0])], axis=0).astype(BF16)
    p_ext = jnp.dot(h_ext, ww_ref[...], preferred_element_type=F32)
    rid = lax.broadcasted_iota(jnp.int32, (tm + 2 * halo, 1), 0)
    inside = jnp.logical_and(jnp.logical_or(rid >= halo, i > 0),
                             jnp.logical_or(rid < tm + halo, i < n - 1))
    ext = jnp.where(inside, p_ext, 0.0)
    pret_ref[0] = jnp.dot(h.astype(BF16), wr_ref[...], preferred_element_type=F32)
    rows = tm + 2 * halo
    p = ext[halo:halo + tm]
    prev = pltpu.roll(ext, 1, axis=0)[halo:halo + tm]
    nxt = pltpu.roll(ext, rows - 1, axis=0)[halo:halo + tm]
    mu_prev, mu_next = mu_ref[0:1, :], mu_ref[1:2, :]
    prw_ref[0] = (1.0 - mu_prev - mu_next) * p + mu_prev * prev + mu_next * nxt


def _inproj(x, shift, scale, g, w_ret, w_rw, mu, tm):
    bsz, t, d = x.shape
    const = lambda b, i: (0, 0)
    blk8 = tm // SUBLANES
    last8 = t // SUBLANES - 1
    return pl.pallas_call(
        _inproj_kernel,
        grid=(bsz, t // tm),
        in_specs=[pl.BlockSpec((1, tm, d), lambda b, i: (b, i, 0)),
                  pl.BlockSpec((1, SUBLANES, d), lambda b, i: (b, jnp.maximum(i * blk8 - 1, 0), 0)),
                  pl.BlockSpec((1, SUBLANES, d), lambda b, i: (b, jnp.minimum((i + 1) * blk8, last8), 0)),
                  pl.BlockSpec((1, 1, d), lambda b, i: (b, 0, 0)),
                  pl.BlockSpec((1, 1, d), lambda b, i: (b, 0, 0)),
                  pl.BlockSpec((1, d), const),
                  pl.BlockSpec(w_ret.shape, const, pipeline_mode=pl.Buffered(1)),
                  pl.BlockSpec(w_rw.shape, const, pipeline_mode=pl.Buffered(1)),
                  pl.BlockSpec(mu.shape, const)],
        out_specs=[pl.BlockSpec((1, tm, RET_COLS), lambda b, i: (b, i, 0)),
                   pl.BlockSpec((1, tm, SHIFT_COLS), lambda b, i: (b, i, 0))],
        out_shape=[jax.ShapeDtypeStruct((bsz, t, RET_COLS), F32),
                   jax.ShapeDtypeStruct((bsz, t, SHIFT_COLS), F32)],
        compiler_params=_params("parallel", "parallel"),
        name="inproj",
    )(x, x, x, shift, scale, g, w_ret, w_rw, mu)


def _ret_kernel(q_ref, k_ref, v_ref, g_ref, kc_ref, vc_ref, cos_ref, sin_ref, ld_ref, o_ref):
    c = RET_CHUNK
    hd = RET_HEAD_DIM
    t = q_ref.shape[1]
    tc = kc_ref.shape[1]
    n = t // c
    k_scale = hd ** -0.5
    lgf = -jnp.exp(ld_ref[0, 0][0:1, :])
    lgb = -jnp.exp(ld_ref[1, 0][0:1, :])

    jc = lax.broadcasted_iota(jnp.int32, (tc, 1), 0).astype(F32)
    kc = kc_ref[0] * k_scale
    vc = vc_ref[0]
    s_f = _mm_tn(kc * jnp.exp(lgf * (tc - 1.0 - jc)), vc)
    s_b = _mm_tn(kc * jnp.exp(lgb * jc), vc)

    def rope(x, cs, sn):
        return x * cs + pltpu.roll(x, hd // 2, axis=1) * sn

    ii = lax.broadcasted_iota(jnp.int32, (c, 1), 0).astype(F32)
    dist = (lax.broadcasted_iota(jnp.int32, (c, c), 0)
            - lax.broadcasted_iota(jnp.int32, (c, c), 1)).astype(F32)
    decay = jnp.where(dist >= 0, jnp.exp(lgf * jnp.maximum(dist, 0.0)),
                      jnp.exp(lgb * jnp.maximum(-dist, 0.0)))
    qdec_f = jnp.exp(lgf * (ii + 1.0))
    kdec_f = jnp.exp(lgf * (c - 1.0 - ii))
    qdec_b = jnp.exp(lgb * (c - ii))
    kdec_b = jnp.exp(lgb * ii)
    cdec_f = jnp.exp(lgf * c)
    cdec_b = jnp.exp(lgb * c)

    chunks = [slice(ci * c, (ci + 1) * c) for ci in range(n)]
    qs = [rope(q_ref[0, sl, :], cos_ref[sl, :], sin_ref[sl, :]) for sl in chunks]
    ks = [rope(k_ref[0, sl, :] * k_scale, cos_ref[sl, :], sin_ref[sl, :]) for sl in chunks]
    scores = [_mm_nt(qc, kx) for qc, kx in zip(qs, ks)]
    kvs = [_mm_tn(jnp.concatenate([kx * kdec_f, kx * kdec_b], axis=1), v_ref[0, sl, :])
           for kx, sl in zip(ks, chunks)]

    sfs, sbs = [None] * n, [None] * n
    for ci in range(n):
        sfs[ci] = s_f
        s_f = s_f * cdec_f + kvs[ci][:hd]
    for ci in range(n - 1, -1, -1):
        sbs[ci] = s_b
        s_b = s_b * cdec_b + kvs[ci][hd:]

    outs = [_mm(jnp.concatenate([sc * decay, qc * qdec_f, qc * qdec_b], axis=1),
                jnp.concatenate([v_ref[0, sl, :].astype(BF16), sf.astype(BF16), sb.astype(BF16)], axis=0))
            for sc, qc, sl, sf, sb in zip(scores, qs, chunks, sfs, sbs)]
    for o, sl in zip(outs, chunks):
        o = o * lax.rsqrt(jnp.mean(o * o, axis=-1, keepdims=True) + NORM_EPS)
        o_ref[0, sl, :] = (o * jax.nn.silu(g_ref[0, sl, :])).astype(o_ref.dtype)


def _retention(p_ret, pc_ret, cos_t, sin_t, ld):
    bsz, t, _ = p_ret.shape
    tc = pc_ret.shape[1]
    hd = RET_HEAD_DIM
    col = lambda j: pl.BlockSpec((1, t, hd), lambda b, h: (b, 0, h + RET_HEADS * j))
    colc = lambda j: pl.BlockSpec((1, tc, hd), lambda b, h: (b, 0, h + RET_HEADS * j))
    return pl.pallas_call(
        _ret_kernel,
        grid=(bsz, RET_HEADS),
        in_specs=[col(0), col(1), col(2), col(3), colc(1), colc(2),
                  pl.BlockSpec((t, hd), lambda b, h: (0, 0)),
                  pl.BlockSpec((t, hd), lambda b, h: (0, 0)),
                  pl.BlockSpec((2, 1, SUBLANES, LANES), lambda b, h: (0, h, 0, 0))],
        out_specs=pl.BlockSpec((1, t, hd), lambda b, h: (b, 0, h)),
        out_shape=jax.ShapeDtypeStruct((bsz, t, RET_WIDTH), BF16),
        compiler_params=_params("parallel", "parallel"),
        name="retention",
    )(p_ret, p_ret, p_ret, p_ret, pc_ret, pc_ret, cos_t, sin_t, ld)


def _group_expand(y, head_of_lane):
    zero = jnp.zeros_like(y)
    return jnp.concatenate([jnp.where(head_of_lane == h, y, zero) for h in range(RWKV_GROUP_HEADS)],
                           axis=0)


def _mm_group(a, y, head_of_lane):
    return _mm(a, _group_expand(y.astype(BF16), head_of_lane))


def _rwkv_scan_kernel(pf_ref, pb_ref, w0_ref, wup_ref, a0_ref, aup_ref, gup_ref, kk_ref, ka_ref, rk_ref,
                      red_ref, bc_ref, s0_ref, yf_ref, yb_ref, bonus_ref, gate_ref, fin_ref, st_s):
    c = RWKV_CHUNK
    w_ = RWKV_WIDTH
    hd = RWKV_HEAD_DIM
    nb = pf_ref.shape[0]
    j = pl.program_id(1)
    n = pl.num_programs(1)

    per_group = RWKV_GROUP_LANES // LANES

    def pair_window(ref, p):
        g, o = divmod(p, per_group)
        return ref.at[:, :, g, o * LANES:(o + 1) * LANES, o * LANES:(o + 1) * LANES]

    @pl.when(j == 0)
    def _():
        st_s[...] = jnp.zeros(st_s.shape, F32)
        for p in range(s0_ref.shape[2]):
            pair_window(st_s, p)[...] = s0_ref[:, :, p]

    row = lax.broadcasted_iota(jnp.int32, (c, c), 0)
    colm = lax.broadcasted_iota(jnp.int32, (c, c), 1)
    gl = RWKV_GROUP_LANES
    lane = lax.broadcasted_iota(jnp.int32, (c, gl), 1)
    trow = lax.broadcasted_iota(jnp.int32, (c, gl), 0)
    head_of = lane // hd
    jj = lane % hd
    r2 = lax.broadcasted_iota(jnp.int32, (gl, gl), 0)
    c2 = lax.broadcasted_iota(jnp.int32, (gl, gl), 1)
    blockdiag = (r2 // hd) == (c2 // hd)
    eye_cat = jnp.where(jj == trow, 1.0, 0.0)
    red = red_ref[...]
    bc2 = bc_ref[...]

    def same_block(size):
        return (trow // size) == (jj // size)

    def stacked(fn, xs):
        out = fn(jnp.concatenate(xs, axis=0))
        return [out[i * c:(i + 1) * c] for i in range(len(xs))]

    def head_sums(xs):
        return stacked(lambda x: jnp.dot(x.astype(BF16), red, preferred_element_type=F32), xs)

    def head_bcast(ss):
        return stacked(lambda s: jnp.dot(jnp.concatenate(_split2(s), axis=1), bc2,
                                         preferred_element_type=F32), ss)

    def prep_stages(rows, out):
        items = [dict(bi=bi, d=d, fwd=d == 0) for bi in rows for d in range(2)]
        for it in items:
            ps = (pf_ref if it["fwd"] else pb_ref)[it["bi"]]
            k = ps[:, w_:2 * w_]
            it.update(ps=ps, k=k, kk0=k * kk_ref[...])
        for it, s in zip(items, head_sums([it["kk0"] * it["kk0"] for it in items])):
            it["nrm"] = s
        for d in range(2):
            sel = [it for it in items if it["d"] == d]
            wls = [jnp.tanh(it["ps"][:, 3 * w_:3 * w_ + DECAY_LORA]) for it in sel]
            als = [it["ps"][:, 3 * w_ + DECAY_LORA:3 * w_ + DECAY_LORA + AAA_LORA] for it in sel]
            for it, wp, ap in zip(sel, stacked(lambda x: _mm(x, wup_ref[d]), wls),
                                  stacked(lambda x: _mm(x, aup_ref[d]), als)):
                it.update(w_pre=wp, a_pre=ap)
        yield
        for it, nb_ in zip(items, head_bcast([lax.rsqrt(it["nrm"] + 1e-12) for it in items])):
            it["inv_norm"] = nb_
        for it in items:
            d = it["d"]
            logw = -W_DECAY_SCALE * jax.nn.sigmoid(w0_ref[d:d + 1, :] + it["w_pre"])
            tri = ((colm <= row) if it["fwd"] else (colm >= row)).astype(BF16)
            it.update(logw=logw, big_l=_mm_exact_rhs(tri, logw),
                      a=jax.nn.sigmoid(a0_ref[d:d + 1, :] + it["a_pre"]))
        yield
        for it in items:
            bi, fwd, ps, k, a, logw, big_l = (it[x] for x in ("bi", "fwd", "ps", "k", "a", "logw", "big_l"))
            r = ps[:, 0:w_]
            v = ps[:, 2 * w_:3 * w_]
            kk = it["kk0"] * it["inv_norm"]
            kt = k * (1.0 + (a - 1.0) * ka_ref[...])
            b = kk * a
            it.update(v=v, rkt=r * kt * rk_ref[...])
            ltot = big_l[c - 1:c, :] if fwd else big_l[0:1, :]
            e_lx = jnp.exp(big_l - logw)
            e_nl = jnp.exp(-big_l)
            e_tot = jnp.exp(ltot)
            e_rest = e_tot * e_nl
            strict = (jj < trow) if fwd else (jj > trow)
            out.append(dict(bi=bi, d=it["d"], v=v, at=-kk * e_lx, bt=b * e_nl, ktl=kt * e_nl,
                            rt=r * (jnp.exp(big_l) if fwd else e_lx), bh=b * e_rest, kh=kt * e_rest,
                            e_tot=e_tot, strict=strict,
                            omask=(jj <= trow) if fwd else strict))
        fw = [it for it in items if it["fwd"]]
        gls = [jax.nn.sigmoid(it["ps"][:, 3 * w_ + DECAY_LORA + AAA_LORA:]) for it in fw]
        for it, g_ in zip(fw, stacked(lambda x: _mm(x, gup_ref[...]), gls)):
            gate_ref[it["bi"]] = g_
        bsums = head_sums([it["rkt"] for it in fw])
        yield
        for it, bb in zip(fw, head_bcast(bsums)):
            bonus_ref[it["bi"]] = bb * it["v"]

    def chain_stages(ops):
        chains = []
        for po in ops:
            bi, d = po["bi"], po["d"]
            for pr in range(RWKV_GROUPS):
                sl = slice(pr * gl, (pr + 1) * gl)
                lhs = jnp.concatenate([po["at"][:, sl], po["rt"][:, sl]], axis=0)
                m = _mm_nt(lhs, jnp.concatenate([_group_expand(po["bt"][:, sl].astype(BF16), head_of),
                                                 _group_expand(po["ktl"][:, sl].astype(BF16), head_of)], axis=0))
                s_in = st_s[bi, d, pr]
                ch = dict(bi=bi, d=d, pr=pr, sl=sl, v=po["v"][:, sl], bh=po["bh"][:, sl],
                          kh=po["kh"][:, sl], e_tot=po["e_tot"][:, sl], s_in=s_in, sx=_mm_nt(lhs, s_in),
                          a_ab=jnp.where(po["strict"], m[:c, :gl], 0.0),
                          a_ak=jnp.where(po["strict"], m[:c, gl:], 0.0),
                          m_rb=jnp.where(po["omask"], m[c:, :gl], 0.0),
                          m_rk=jnp.where(po["omask"], m[c:, gl:], 0.0))
                chains.append(ch)
        yield

        a_diag = [jnp.where(same_block(INV_BASE_BLOCK), ch["a_ab"], 0.0) for ch in chains]
        tinv = [eye_cat + a for a in a_diag]
        pw = [_mm_group(p, p, head_of) for p in a_diag]
        yield
        span = 4
        while span < INV_BASE_BLOCK:
            both = [_mm_group(jnp.concatenate([t, p], axis=0), p, head_of) for t, p in zip(tinv, pw)]
            tinv = [t + r[:c] for t, r in zip(tinv, both)]
            pw = [r[c:] for r in both]
            yield
            span *= 2
        tinv = [t + _mm_group(t, p, head_of) for t, p in zip(tinv, pw)]
        yield
        size = INV_BASE_BLOCK
        while size < c:
            off = jnp.logical_and(same_block(2 * size), jnp.logical_not(same_block(size)))
            ta = [_mm_group(t, jnp.where(off, ch["a_ab"], 0.0), head_of) for t, ch in zip(tinv, chains)]
            yield
            tinv = [t + _mm_group(x, t, head_of) for t, x in zip(tinv, ta)]
            yield
            size *= 2

        avs = [_mm(jnp.concatenate([ch["a_ak"], ch["m_rk"]], axis=0),
                   _group_expand(ch["v"].astype(BF16), head_of)) for ch in chains]
        yield
        us = [_mm_group(t_m, ch["sx"][:c] + av[:c], head_of) for ch, t_m, av in zip(chains, tinv, avs)]
        yield
        for ch, av, u in zip(chains, avs, us):
            y_ref = yf_ref if ch["d"] == 0 else yb_ref
            y_ref[ch["bi"], :, ch["sl"]] = ch["sx"][c:] + av[c:] + _mm_group(ch["m_rb"], u, head_of)
        upds = [_mm_tn(jnp.concatenate([u, ch["v"]], axis=0), jnp.concatenate([ch["bh"], ch["kh"]], axis=0))
                for ch, u in zip(chains, us)]
        yield
        for ch, upd in zip(chains, upds):
            st_s[ch["bi"], ch["d"], ch["pr"]] = ch["s_in"] * ch["e_tot"] + jnp.where(blockdiag, upd, 0.0)

    groups = [range(g, min(g + RWKV_CHAIN_ROWS, nb)) for g in range(0, nb, RWKV_CHAIN_ROWS)]
    ops = []
    for _ in prep_stages(groups[0], ops):
        pass
    for gi in range(len(groups)):
        nxt = []
        prep = prep_stages(groups[gi + 1], nxt) if gi + 1 < len(groups) else iter(())
        for lvl, _ in enumerate(chain_stages(ops)):
            if lvl % PREP_EVERY_CHAIN_STAGES == 0:
                next(prep, None)
        for _ in prep:
            pass
        ops = nxt

    @pl.when(j == n - 1)
    def _():
        for p in range(fin_ref.shape[2]):
            fin_ref[:, :, p] = pair_window(st_s, p)[...]


def _rwkv_scan(p_rw, s0, w0, wup, a0, aup, gup, k_k, k_a, r_k, red, bc2):
    bsz, t, cols = p_rw.shape
    c = RWKV_CHUNK
    nb = RWKV_BATCH_BLOCK
    n = t // c
    const2 = lambda b, i: (0, 0)
    const3 = lambda b, i: (0, 0, 0)
    state_blk = (nb, 2, RWKV_STATE_PAIRS, LANES, LANES)
    state_map = lambda b, i: (b, 0, 0, 0, 0)
    state_shape = jax.ShapeDtypeStruct((bsz, 2, RWKV_STATE_PAIRS, LANES, LANES), F32)
    tok_shape = jax.ShapeDtypeStruct((bsz, t, RWKV_WIDTH), F32)
    tok_f = pl.BlockSpec((nb, c, RWKV_WIDTH), lambda b, i: (b, i, 0))
    tok_b = pl.BlockSpec((nb, c, RWKV_WIDTH), lambda b, i: (b, n - 1 - i, 0))
    return pl.pallas_call(
        _rwkv_scan_kernel,
        grid=(bsz // nb, n),
        in_specs=[
            pl.BlockSpec((nb, c, cols), lambda b, i: (b, i, 0)),
            pl.BlockSpec((nb, c, cols), lambda b, i: (b, n - 1 - i, 0)),
            pl.BlockSpec(w0.shape, const2),
            pl.BlockSpec(wup.shape, const3), pl.BlockSpec(a0.shape, const2),
            pl.BlockSpec(aup.shape, const3), pl.BlockSpec(gup.shape, const2),
            pl.BlockSpec(k_k.shape, const2), pl.BlockSpec(k_a.shape, const2),
            pl.BlockSpec(r_k.shape, const2), pl.BlockSpec(red.shape, const2),
            pl.BlockSpec(bc2.shape, const2),
            pl.BlockSpec(state_blk, state_map, pipeline_mode=pl.Buffered(1))],
        out_specs=[tok_f, tok_b, tok_f, tok_f, pl.BlockSpec(state_blk, state_map)],
        out_shape=[tok_shape] * 4 + [state_shape],
        scratch_shapes=[pltpu.VMEM((nb, 2, RWKV_GROUPS, RWKV_GROUP_LANES, RWKV_GROUP_LANES), F32)],
        compiler_params=_params("parallel", "arbitrary"),
        name="rwkv_scan",
    )(p_rw, p_rw, w0, wup, a0, aup, gup, k_k, k_a, r_k, red, bc2, s0)


def _mlp_kernel(x_ref, ret_ref, yf_ref, yb_ref, bonus_ref, gate_ref, lnw_ref, lnb_ref, red_ref, bc_ref,
                g1_ref, sh2_ref, sc2_ref, g2_ref, n2g_ref, fg_ref,
                woa_ref, wob_ref, w1_ref, b1_ref, w2_ref, b2_ref, o_ref, *, ff_tile):
    mix_ret = jnp.dot(ret_ref[0], woa_ref[...], preferred_element_type=F32)
    red = red_ref[...]
    bc2 = bc_ref[...]
    inv_n = 1.0 / RWKV_HEAD_DIM
    blk = MERGE_ROW_BLOCK
    rows = [slice(r, r + blk) for r in range(0, x_ref.shape[1], blk)]

    def head_means(xs, two_piece):
        if two_piece:
            ss = [jnp.dot(jnp.concatenate(_split2(x), axis=0), red, preferred_element_type=F32) for x in xs]
            ss = [s[:blk] + s[blk:] for s in ss]
        else:
            ss = [jnp.dot(x.astype(BF16), red, preferred_element_type=F32) for x in xs]
        return [jnp.dot(jnp.concatenate(_split2(s * inv_n), axis=1), bc2, preferred_element_type=F32)
                for s in ss]

    ys = [yf_ref[0, r, :] + yb_ref[0, r, :] for r in rows]
    ycs = [y - m for y, m in zip(ys, head_means(ys, two_piece=True))]
    variances = head_means([yc * yc for yc in ycs], two_piece=False)
    rw = jnp.concatenate(
        [((yc * lax.rsqrt(var + GN_EPS) * lnw_ref[...] + lnb_ref[...] + bonus_ref[0, r, :])
          * gate_ref[0, r, :]).astype(BF16) for yc, var, r in zip(ycs, variances, rows)], axis=0)
    mix = mix_ret + jnp.dot(rw, wob_ref[...], preferred_element_type=F32)
    h1 = x_ref[0] + g1_ref[0] * mix
    n2 = (_rms(h1, n2g_ref[...] * (1.0 + sc2_ref[0])) + sh2_ref[0]).astype(BF16)
    d_ff = w1_ref.shape[1]
    slices = [slice(s * ff_tile, (s + 1) * ff_tile) for s in range(d_ff // ff_tile)]

    def up(sl):
        return jnp.dot(n2, w1_ref[:, sl], preferred_element_type=F32) + b1_ref[:, sl]

    acc = jnp.zeros(h1.shape, F32)
    hid = up(slices[0])
    for s, sl in enumerate(slices):
        nxt = up(slices[s + 1]) if s + 1 < len(slices) else None
        act = jnp.square(jnp.maximum(hid, 0.0)).astype(BF16)
        acc = acc + jnp.dot(act, w2_ref[sl, :], preferred_element_type=F32)
        hid = nxt
    h2 = h1 + g2_ref[0] * (acc + b2_ref[...])
    o_ref[0] = _rms(h2, fg_ref[...])


def _mlp(x, ret, yf, yb, bonus, gate, ln_w, ln_b, red, bc2, g1, sh2, sc2, g2, n2g, fg,
         woa, wob, w1, b1, w2, b2, tm):
    bsz, t, d = x.shape
    const = lambda b, i: (0, 0)
    single = pl.Buffered(1)
    mod = pl.BlockSpec((1, 1, d), lambda b, i: (b, 0, 0))
    cw = lambda a: pl.BlockSpec(a.shape, const, pipeline_mode=single)
    tok = lambda w: pl.BlockSpec((1, tm, w), lambda b, i: (b, i, 0))
    return pl.pallas_call(
        functools.partial(_mlp_kernel, ff_tile=1024),
        grid=(bsz, t // tm),
        in_specs=[tok(d), tok(RET_WIDTH), tok(RWKV_WIDTH), tok(RWKV_WIDTH), tok(RWKV_WIDTH),
                  tok(RWKV_WIDTH), cw(ln_w), cw(ln_b), cw(red), cw(bc2),
                  mod, mod, mod, mod, cw(n2g), cw(fg),
                  cw(woa), cw(wob), cw(w1), cw(b1), cw(w2), cw(b2)],
        out_specs=tok(d),
        out_shape=jax.ShapeDtypeStruct((bsz, t, d), F32),
        compiler_params=_params("parallel", "parallel"),
        name="outproj_mlp",
    )(x, ret, yf, yb, bonus, gate, ln_w, ln_b, red, bc2, g1, sh2, sc2, g2, n2g, fg,
      woa, wob, w1, b1, w2, b2)


def _rope_tables(t):
    half = RET_HEAD_DIM // 2
    inv = jnp.power(ROPE_BASE, -jnp.arange(0, half, 2, dtype=F32) / half)
    pos = jnp.arange(t)
    ang_r = (pos // GRID_W).astype(F32)[:, None] * inv[None, :]
    ang_c = (pos % GRID_W).astype(F32)[:, None] * inv[None, :]
    cos_t = jnp.concatenate([jnp.cos(ang_r), jnp.cos(ang_c)] * 2, axis=-1)
    sin_t = jnp.concatenate([-jnp.sin(ang_r), -jnp.sin(ang_c), jnp.sin(ang_r), jnp.sin(ang_c)], axis=-1)
    return cos_t, sin_t


def _rope_column_order(w_ret):
    d = w_ret.shape[0]
    quarter = RET_HEAD_DIM // 4
    qk = w_ret[:, :2 * RET_WIDTH].reshape(d, 2 * RET_HEADS, 2, 2, quarter)
    qk = jnp.swapaxes(qk, 2, 3).reshape(d, 2 * RET_WIDTH)
    return jnp.concatenate([qk, w_ret[:, 2 * RET_WIDTH:]], axis=1)


def kernel(x, c, ctx, c_ctx, w_ada, b_ada, norm1_g, norm2_g, w_in, ret_log_decay, rwkv_shift_mu,
           rwkv_w0, rwkv_w_up, rwkv_a0, rwkv_a_up, rwkv_g_up, rwkv_k_k, rwkv_k_a, rwkv_r_k,
           rwkv_ln_w, rwkv_ln_b, w_out, w_ff1, b_ff1, w_ff2, b_ff2, final_g):
    bsz, t, d = x.shape
    assert w_ada.shape[0] == 1, "single trunk layer"
    assert bsz % RWKV_BATCH_BLOCK == 0
    l = 0

    rows = -(-(bsz + 1) // SUBLANES) * SUBLANES
    cc = jnp.zeros((rows, d), F32).at[:bsz].set(c).at[bsz].set(c_ctx)
    mods = _ada(cc, w_ada[l], b_ada[l][None, :])
    sh1, sc1, g1, sh2, sc2, g2 = [m[:bsz, None, :] for m in jnp.split(mods, 6, axis=-1)]
    csh1, csc1 = [jnp.broadcast_to(m[bsz][None, None, :], (bsz, 1, d))
                  for m in jnp.split(mods, 6, axis=-1)[:2]]

    w_ret = _rope_column_order(w_in[l][:, :RET_COLS]).astype(BF16)
    w_rw = w_in[l][:, RET_COLS:].astype(BF16)
    n1g = norm1_g[l][None, :]
    mu = rwkv_shift_mu[l]
    p_ret, p_rw = _inproj(x, sh1, sc1, n1g, w_ret, w_rw, mu, tm=TOKEN_TILE)
    pc_ret, pc_rw = _inproj(ctx, csh1, csc1, n1g, w_ret, w_rw, mu, tm=ctx.shape[1])

    cos_t, sin_t = _rope_tables(t)
    ld = jnp.broadcast_to(ret_log_decay[l][:, :, None, None], (2, RET_HEADS, SUBLANES, LANES))
    ret_out = _retention(p_ret, pc_ret, cos_t, sin_t, ld)

    head_id = jnp.arange(RWKV_WIDTH) // RWKV_HEAD_DIM
    red = (head_id[:, None] == jnp.arange(LANES)[None, :]).astype(BF16)
    bc2 = jnp.concatenate([red.T, red.T], axis=0)
    rw_params = (rwkv_w0[l], rwkv_w_up[l].astype(BF16), rwkv_a0[l],
                 rwkv_a_up[l].astype(BF16), rwkv_g_up[l].astype(BF16),
                 rwkv_k_k[l][None, :], rwkv_k_a[l][None, :], rwkv_r_k[l][None, :], red, bc2)
    zeros = jnp.zeros((bsz, 2, RWKV_STATE_PAIRS, LANES, LANES), F32)
    s_ctx = _rwkv_scan(pc_rw, zeros, *rw_params)[4]
    yf, yb, bonus, gate, _ = _rwkv_scan(p_rw, s_ctx, *rw_params)

    wo = w_out[l].astype(BF16)
    return _mlp(x, ret_out, yf, yb, bonus, gate, rwkv_ln_w[l][None, :], rwkv_ln_b[l][None, :], red, bc2,
                g1, sh2, sc2, g2, norm2_g[l][None, :], final_g[None, :],
                wo[:RET_WIDTH], wo[RET_WIDTH:], w_ff1[l].astype(BF16), b_ff1[l][None, :],
                w_ff2[l].astype(BF16), b_ff2[l][None, :], tm=TOKEN_TILE)
```

```python
import functools
import math

import jax
import jax.numpy as jnp
from jax import lax
from jax.experimental import pallas as pl
from jax.experimental.pallas import tpu as pltpu

F32 = jnp.float32
BF16 = jnp.bfloat16

GRID_W = 64
RET_HEADS = 4
RET_HEAD_DIM = 128
RET_WIDTH = RET_HEADS * RET_HEAD_DIM
RET_CHUNK = 128
RET_HEADS_PER_STEP = 2
RWKV_HEAD_DIM = 64
RWKV_WIDTH = 512
RWKV_GROUP_HEADS = 2
RWKV_GROUP_LANES = RWKV_GROUP_HEADS * RWKV_HEAD_DIM
RWKV_GROUPS = RWKV_WIDTH // RWKV_GROUP_LANES
RWKV_STATE_PAIRS = RWKV_WIDTH // 128
RWKV_CHUNK = 64
RWKV_BATCH_BLOCK = 8
RWKV_CHAIN_ROWS = 2
PREP_EVERY_CHAIN_STAGES = 3
INV_BASE_BLOCK = 16
DECAY_LORA = 64
AAA_LORA = 64
GATE_LORA = 128
SHIFT_COLS = 3 * RWKV_WIDTH + DECAY_LORA + AAA_LORA + GATE_LORA
RET_COLS = 4 * RET_WIDTH
ROPE_BASE = 10000.0
NORM_EPS = 1e-6
GN_EPS = 64e-5
W_DECAY_SCALE = math.exp(-0.5)
LANES = 128
SUBLANES = 8
TOKEN_TILE = 512
MERGE_ROW_BLOCK = 128
VMEM_LIMIT = 56 * 1024 * 1024


def _params(*sem):
    return pltpu.CompilerParams(dimension_semantics=sem, vmem_limit_bytes=VMEM_LIMIT)


def _mm(a, b):
    return jnp.dot(a.astype(BF16), b.astype(BF16), preferred_element_type=F32)


def _mm_nt(a, b):
    return lax.dot_general(a.astype(BF16), b.astype(BF16), (((1,), (1,)), ((), ())),
                           preferred_element_type=F32)


def _mm_tn(a, b):
    return lax.dot_general(a.astype(BF16), b.astype(BF16), (((0,), (0,)), ((), ())),
                           preferred_element_type=F32)


def _split3(x):
    hi = x.astype(BF16)
    r1 = x - hi.astype(F32)
    mid = r1.astype(BF16)
    lo = (r1 - mid.astype(F32)).astype(BF16)
    return hi, mid, lo


def _mm_exact_rhs(a_bf16, x):
    return jnp.dot(jnp.concatenate([a_bf16] * 3, axis=1), jnp.concatenate(_split3(x), axis=0),
                   preferred_element_type=F32)


def _split2(x):
    hi = x.astype(BF16)
    return hi, (x - hi.astype(F32)).astype(BF16)


def _rms(x, g):
    return x * lax.rsqrt(jnp.mean(x * x, axis=-1, keepdims=True) + NORM_EPS) * g


def _ada_kernel(c_ref, w_ref, b_ref, o_ref):
    s = jax.nn.silu(c_ref[...])
    o_ref[...] = jnp.dot(s, w_ref[...], preferred_element_type=F32,
                         precision=lax.Precision.HIGHEST) + b_ref[...]


def _ada(cc, w, b):
    rows, d = cc.shape
    n = w.shape[1]
    tn = n // 4
    return pl.pallas_call(
        _ada_kernel,
        grid=(n // tn,),
        in_specs=[pl.BlockSpec((rows, d), lambda j: (0, 0)),
                  pl.BlockSpec((d, tn), lambda j: (0, j)),
                  pl.BlockSpec((1, tn), lambda j: (0, j))],
        out_specs=pl.BlockSpec((rows, tn), lambda j: (0, j)),
        out_shape=jax.ShapeDtypeStruct((rows, n), F32),
        compiler_params=_params("arbitrary"),
        name="adaln",
    )(cc, w, b)


def _inproj_kernel(x_ref, xp_ref, xn_ref, sh_ref, sc_ref, g_ref, wr_ref, ww_ref, mu_ref,
                   pret_ref, prw_ref):
    i = pl.program_id(1)
    n = pl.num_programs(1)
    tm = x_ref.shape[1]
    halo = SUBLANES

    gain = g_ref[...] * (1.0 + sc_ref[0])

    def modulated(xv):
        return xv * lax.rsqrt(jnp.mean(xv * xv, axis=-1, keepdims=True) + NORM_EPS) * gain + sh_ref[0]

    h = modulated(x_ref[0])
    h_ext = jnp.concatenate([modulated(xp_ref[0]), h, modulated(xn_ref[0])], axis=0).astype(BF16)
    p_ext = jnp.dot(h_ext, ww_ref[...], preferred_element_type=F32)
    rid = lax.broadcasted_iota(jnp.int32, (tm + 2 * halo, 1), 0)
    inside = jnp.logical_and(jnp.logical_or(rid >= halo, i > 0),
                             jnp.logical_or(rid < tm + halo, i < n - 1))
    ext = jnp.where(inside, p_ext, 0.0)
    pret_ref[0] = jnp.dot(h.astype(BF16), wr_ref[...], preferred_element_type=F32)
    rows = tm + 2 * halo
    p = ext[halo:halo + tm]
    prev = pltpu.roll(ext, 1, axis=0)[halo:halo + tm]
    nxt = pltpu.roll(ext, rows - 1, axis=0)[halo:halo + tm]
    mu_prev, mu_next = mu_ref[0:1, :], mu_ref[1:2, :]
    prw_ref[0] = (1.0 - mu_prev - mu_next) * p + mu_prev * prev + mu_next * nxt


def _inproj(x, shift, scale, g, w_ret, w_rw, mu, tm):
    bsz, t, d = x.shape
    const = lambda b, i: (0, 0)
    blk8 = tm // SUBLANES
    last8 = t // SUBLANES - 1
    return pl.pallas_call(
        _inproj_kernel,
        grid=(bsz, t // tm),
        in_specs=[pl.BlockSpec((1, tm, d), lambda b, i: (b, i, 0)),
                  pl.BlockSpec((1, SUBLANES, d), lambda b, i: (b, jnp.maximum(i * blk8 - 1, 0), 0)),
                  pl.BlockSpec((1, SUBLANES, d), lambda b, i: (b, jnp.minimum((i + 1) * blk8, last8), 0)),
                  pl.BlockSpec((1, 1, d), lambda b, i: (b, 0, 0)),
                  pl.BlockSpec((1, 1, d), lambda b, i: (b, 0, 0)),
                  pl.BlockSpec((1, d), const),
                  pl.BlockSpec(w_ret.shape, const, pipeline_mode=pl.Buffered(1)),
                  pl.BlockSpec(w_rw.shape, const, pipeline_mode=pl.Buffered(1)),
                  pl.BlockSpec(mu.shape, const)],
        out_specs=[pl.BlockSpec((1, tm, RET_COLS), lambda b, i: (b, i, 0)),
                   pl.BlockSpec((1, tm, SHIFT_COLS), lambda b, i: (b, i, 0))],
        out_shape=[jax.ShapeDtypeStruct((bsz, t, RET_COLS), F32),
                   jax.ShapeDtypeStruct((bsz, t, SHIFT_COLS), F32)],
        compiler_params=_params("parallel", "parallel"),
        name="inproj",
    )(x, x, x, shift, scale, g, w_ret, w_rw, mu)


def _ret_kernel(q_ref, k_ref, v_ref, g_ref, kc_ref, vc_ref, cos_ref, sin_ref, ld_ref, o_ref):
    c = RET_CHUNK
    hd = RET_HEAD_DIM
    t = q_ref.shape[1]
    tc = kc_ref.shape[1]
    n = t // c
    heads = q_ref.shape[2] // hd
    k_scale = hd ** -0.5

    def rope(x, cs, sn):
        return x * cs + pltpu.roll(x, hd // 2, axis=1) * sn

    jc = lax.broadcasted_iota(jnp.int32, (tc, 1), 0).astype(F32)
    ii = lax.broadcasted_iota(jnp.int32, (c, 1), 0).astype(F32)
    dist = (lax.broadcasted_iota(jnp.int32, (c, c), 0)
            - lax.broadcasted_iota(jnp.int32, (c, c), 1)).astype(F32)
    chunks = [slice(ci * c, (ci + 1) * c) for ci in range(n)]

    hs = []
    for h in range(heads):
        ls = slice(h * hd, (h + 1) * hd)
        lgf = -jnp.exp(ld_ref[0, h][0:1, :])
        lgb = -jnp.exp(ld_ref[1, h][0:1, :])
        kc = kc_ref[0, :, ls] * k_scale
        vc = vc_ref[0, :, ls]
        hs.append(dict(
            ls=ls, s_f=_mm_tn(kc * jnp.exp(lgf * (tc - 1.0 - jc)), vc), s_b=_mm_tn(kc * jnp.exp(lgb * jc), vc),
            decay=jnp.where(dist >= 0, jnp.exp(lgf * jnp.maximum(dist, 0.0)),
                            jnp.exp(lgb * jnp.maximum(-dist, 0.0))),
            qdec_f=jnp.exp(lgf * (ii + 1.0)), kdec_f=jnp.exp(lgf * (c - 1.0 - ii)),
            qdec_b=jnp.exp(lgb * (c - ii)), kdec_b=jnp.exp(lgb * ii),
            cdec_f=jnp.exp(lgf * c), cdec_b=jnp.exp(lgb * c)))

    for hh in hs:
        ls = hh["ls"]
        hh["qs"] = [rope(q_ref[0, sl, ls], cos_ref[sl, :], sin_ref[sl, :]) for sl in chunks]
        hh["ks"] = [rope(k_ref[0, sl, ls] * k_scale, cos_ref[sl, :], sin_ref[sl, :]) for sl in chunks]
    for hh in hs:
        hh["scores"] = [_mm_nt(qc, kx) for qc, kx in zip(hh["qs"], hh["ks"])]
    for hh in hs:
        hh["kvs"] = [_mm_tn(jnp.concatenate([kx * hh["kdec_f"], kx * hh["kdec_b"]], axis=1),
                            v_ref[0, sl, hh["ls"]]) for kx, sl in zip(hh["ks"], chunks)]

    for hh in hs:
        sfs, sbs = [None] * n, [None] * n
        s_f, s_b = hh["s_f"], hh["s_b"]
        for ci in range(n):
            sfs[ci] = s_f
            s_f = s_f * hh["cdec_f"] + hh["kvs"][ci][:hd]
        for ci in range(n - 1, -1, -1):
            sbs[ci] = s_b
            s_b = s_b * hh["cdec_b"] + hh["kvs"][ci][hd:]
        hh.update(sfs=sfs, sbs=sbs)

    for hh in hs:
        hh["outs"] = [
            _mm(jnp.concatenate([sc * hh["decay"], qc * hh["qdec_f"], qc * hh["qdec_b"]], axis=1),
                jnp.concatenate([v_ref[0, sl, hh["ls"]].astype(BF16), sf.astype(BF16), sb.astype(BF16)], axis=0))
            for sc, qc, sl, sf, sb in zip(hh["scores"], hh["qs"], chunks, hh["sfs"], hh["sbs"])]
    for hh in hs:
        for o, sl in zip(hh["outs"], chunks):
            o = o * lax.rsqrt(jnp.mean(o * o, axis=-1, keepdims=True) + NORM_EPS)
            o_ref[0, sl, hh["ls"]] = (o * jax.nn.silu(g_ref[0, sl, hh["ls"]])).astype(o_ref.dtype)


def _retention(p_ret, pc_ret, cos_t, sin_t, ld):
    bsz, t, _ = p_ret.shape
    tc = pc_ret.shape[1]
    hd = RET_HEAD_DIM
    hp = RET_HEADS_PER_STEP
    steps = RET_HEADS // hp
    col = lambda j: pl.BlockSpec((1, t, hp * hd), lambda b, h: (b, 0, h + steps * j))
    colc = lambda j: pl.BlockSpec((1, tc, hp * hd), lambda b, h: (b, 0, h + steps * j))
    return pl.pallas_call(
        _ret_kernel,
        grid=(bsz, steps),
        in_specs=[col(0), col(1), col(2), col(3), colc(1), colc(2),
                  pl.BlockSpec((t, hd), lambda b, h: (0, 0)),
                  pl.BlockSpec((t, hd), lambda b, h: (0, 0)),
                  pl.BlockSpec((2, hp, SUBLANES, LANES), lambda b, h: (0, h, 0, 0))],
        out_specs=pl.BlockSpec((1, t, hp * hd), lambda b, h: (b, 0, h)),
        out_shape=jax.ShapeDtypeStruct((bsz, t, RET_WIDTH), BF16),
        compiler_params=_params("parallel", "parallel"),
        name="retention",
    )(p_ret, p_ret, p_ret, p_ret, pc_ret, pc_ret, cos_t, sin_t, ld)


def _group_expand(y, head_of_lane):
    zero = jnp.zeros_like(y)
    return jnp.concatenate([jnp.where(head_of_lane == h, y, zero) for h in range(RWKV_GROUP_HEADS)],
                           axis=0)


def _mm_group(a, y, head_of_lane):
    return _mm(a, _group_expand(y.astype(BF16), head_of_lane))


def _rwkv_scan_kernel(pf_ref, pb_ref, w0_ref, wup_ref, a0_ref, aup_ref, gup_ref, kk_ref, ka_ref, rk_ref,
                      red_ref, bc_ref, s0_ref, yf_ref, yb_ref, bonus_ref, gate_ref, fin_ref, st_s):
    c = RWKV_CHUNK
    w_ = RWKV_WIDTH
    hd = RWKV_HEAD_DIM
    nb = pf_ref.shape[0]
    j = pl.program_id(1)
    n = pl.num_programs(1)

    per_group = RWKV_GROUP_LANES // LANES

    def pair_window(ref, p):
        g, o = divmod(p, per_group)
        return ref.at[:, :, g, o * LANES:(o + 1) * LANES, o * LANES:(o + 1) * LANES]

    @pl.when(j == 0)
    def _():
        st_s[...] = jnp.zeros(st_s.shape, F32)
        for p in range(s0_ref.shape[2]):
            pair_window(st_s, p)[...] = s0_ref[:, :, p]

    row = lax.broadcasted_iota(jnp.int32, (c, c), 0)
    colm = lax.broadcasted_iota(jnp.int32, (c, c), 1)
    gl = RWKV_GROUP_LANES
    lane = lax.broadcasted_iota(jnp.int32, (c, gl), 1)
    trow = lax.broadcasted_iota(jnp.int32, (c, gl), 0)
    head_of = lane // hd
    jj = lane % hd
    r2 = lax.broadcasted_iota(jnp.int32, (gl, gl), 0)
    c2 = lax.broadcasted_iota(jnp.int32, (gl, gl), 1)
    blockdiag = (r2 // hd) == (c2 // hd)
    eye_cat = jnp.where(jj == trow, 1.0, 0.0)
    red = red_ref[...]
    bc2 = bc_ref[...]

    def same_block(size):
        return (trow // size) == (jj // size)

    def stacked(fn, xs):
        out = fn(jnp.concatenate(xs, axis=0))
        return [out[i * c:(i + 1) * c] for i in range(len(xs))]

    def head_sums(xs):
        return stacked(lambda x: jnp.dot(x.astype(BF16), red, preferred_element_type=F32), xs)

    def head_bcast(ss):
        return stacked(lambda s: jnp.dot(jnp.concatenate(_split2(s), axis=1), bc2,
                                         preferred_element_type=F32), ss)

    def prep_stages(rows, out):
        items = [dict(bi=bi, d=d, fwd=d == 0) for bi in rows for d in range(2)]
        for it in items:
            ps = (pf_ref if it["fwd"] else pb_ref)[it["bi"]]
            k = ps[:, w_:2 * w_]
            it.update(ps=ps, k=k, kk0=k * kk_ref[...])
        for it, s in zip(items, head_sums([it["kk0"] * it["kk0"] for it in items])):
            it["nrm"] = s
        for d in range(2):
            sel = [it for it in items if it["d"] == d]
            wls = [jnp.tanh(it["ps"][:, 3 * w_:3 * w_ + DECAY_LORA]) for it in sel]
            als = [it["ps"][:, 3 * w_ + DECAY_LORA:3 * w_ + DECAY_LORA + AAA_LORA] for it in sel]
            for it, wp, ap in zip(sel, stacked(lambda x: _mm(x, wup_ref[d]), wls),
                                  stacked(lambda x: _mm(x, aup_ref[d]), als)):
                it.update(w_pre=wp, a_pre=ap)
        yield
        for it, nb_ in zip(items, head_bcast([lax.rsqrt(it["nrm"] + 1e-12) for it in items])):
            it["inv_norm"] = nb_
        for it in items:
            d = it["d"]
            logw = -W_DECAY_SCALE * jax.nn.sigmoid(w0_ref[d:d + 1, :] + it["w_pre"])
            tri = ((colm <= row) if it["fwd"] else (colm >= row)).astype(BF16)
            it.update(logw=logw, big_l=_mm_exact_rhs(tri, logw),
                      a=jax.nn.sigmoid(a0_ref[d:d + 1, :] + it["a_pre"]))
        yield
        for it in items:
            bi, fwd, ps, k, a, logw, big_l = (it[x] for x in ("bi", "fwd", "ps", "k", "a", "logw", "big_l"))
            r = ps[:, 0:w_]
            v = ps[:, 2 * w_:3 * w_]
            kk = it["kk0"] * it["inv_norm"]
            kt = k * (1.0 + (a - 1.0) * ka_ref[...])
            b = kk * a
            it.update(v=v, rkt=r * kt * rk_ref[...])
            ltot = big_l[c - 1:c, :] if fwd else big_l[0:1, :]
            e_lx = jnp.exp(big_l - logw)
            e_nl = jnp.exp(-big_l)
            e_tot = jnp.exp(ltot)
            e_rest = e_tot * e_nl
            strict = (jj < trow) if fwd else (jj > trow)
            out.append(dict(bi=bi, d=it["d"], v=v, at=-kk * e_lx, bt=b * e_nl, ktl=kt * e_nl,
                            rt=r * (jnp.exp(big_l) if fwd else e_lx), bh=b * e_rest, kh=kt * e_rest,
                            e_tot=e_tot, strict=strict,
                            omask=(jj <= trow) if fwd else strict))
        fw = [it for it in items if it["fwd"]]
        gls = [jax.nn.sigmoid(it["ps"][:, 3 * w_ + DECAY_LORA + AAA_LORA:]) for it in fw]
        for it, g_ in zip(fw, stacked(lambda x: _mm(x, gup_ref[...]), gls)):
            gate_ref[it["bi"]] = g_
        bsums = head_sums([it["rkt"] for it in fw])
        yield
        for it, bb in zip(fw, head_bcast(bsums)):
            bonus_ref[it["bi"]] = bb * it["v"]

    def chain_stages(ops):
        chains = []
        for po in ops:
            bi, d = po["bi"], po["d"]
            for pr in range(RWKV_GROUPS):
                sl = slice(pr * gl, (pr + 1) * gl)
                lhs = jnp.concatenate([po["at"][:, sl], po["rt"][:, sl]], axis=0)
                m = _mm_nt(lhs, jnp.concatenate([_group_expand(po["bt"][:, sl].astype(BF16), head_of),
                                                 _group_expand(po["ktl"][:, sl].astype(BF16), head_of)], axis=0))
                s_in = st_s[bi, d, pr]
                ch = dict(bi=bi, d=d, pr=pr, sl=sl, v=po["v"][:, sl], bh=po["bh"][:, sl],
                          kh=po["kh"][:, sl], e_tot=po["e_tot"][:, sl], s_in=s_in, sx=_mm_nt(lhs, s_in),
                          a_ab=jnp.where(po["strict"], m[:c, :gl], 0.0),
                          a_ak=jnp.where(po["strict"], m[:c, gl:], 0.0),
                          m_rb=jnp.where(po["omask"], m[c:, :gl], 0.0),
                          m_rk=jnp.where(po["omask"], m[c:, gl:], 0.0))
                chains.append(ch)
        yield

        a_diag = [jnp.where(same_block(INV_BASE_BLOCK), ch["a_ab"], 0.0) for ch in chains]
        tinv = [eye_cat + a for a in a_diag]
        pw = [_mm_group(p, p, head_of) for p in a_diag]
        yield
        span = 4
        while span < INV_BASE_BLOCK:
            both = [_mm_group(jnp.concatenate([t, p], axis=0), p, head_of) for t, p in zip(tinv, pw)]
            tinv = [t + r[:c] for t, r in zip(tinv, both)]
            pw = [r[c:] for r in both]
            yield
            span *= 2
        tinv = [t + _mm_group(t, p, head_of) for t, p in zip(tinv, pw)]
        yield
        size = INV_BASE_BLOCK
        while size < c:
            off = jnp.logical_and(same_block(2 * size), jnp.logical_not(same_block(size)))
            ta = [_mm_group(t, jnp.where(off, ch["a_ab"], 0.0), head_of) for t, ch in zip(tinv, chains)]
            yield
            tinv = [t + _mm_group(x, t, head_of) for t, x in zip(tinv, ta)]
            yield
            size *= 2

        avs = [_mm(jnp.concatenate([ch["a_ak"], ch["m_rk"]], axis=0),
                   _group_expand(ch["v"].astype(BF16), head_of)) for ch in chains]
        yield
        us = [_mm_group(t_m, ch["sx"][:c] + av[:c], head_of) for ch, t_m, av in zip(chains, tinv, avs)]
        yield
        for ch, av, u in zip(chains, avs, us):
            y_ref = yf_ref if ch["d"] == 0 else yb_ref
            y_ref[ch["bi"], :, ch["sl"]] = ch["sx"][c:] + av[c:] + _mm_group(ch["m_rb"], u, head_of)
        upds = [_mm_tn(jnp.concatenate([u, ch["v"]], axis=0), jnp.concatenate([ch["bh"], ch["kh"]], axis=0))
                for ch, u in zip(chains, us)]
        yield
        for ch, upd in zip(chains, upds):
            st_s[ch["bi"], ch["d"], ch["pr"]] = ch["s_in"] * ch["e_tot"] + jnp.where(blockdiag, upd, 0.0)

    groups = [range(g, min(g + RWKV_CHAIN_ROWS, nb)) for g in range(0, nb, RWKV_CHAIN_ROWS)]
    ops = []
    for _ in prep_stages(groups[0], ops):
        pass
    for gi in range(len(groups)):
        nxt = []
        prep = prep_stages(groups[gi + 1], nxt) if gi + 1 < len(groups) else iter(())
        for lvl, _ in enumerate(chain_stages(ops)):
            if lvl % PREP_EVERY_CHAIN_STAGES == 0:
                next(prep, None)
        for _ in prep:
            pass
        ops = nxt

    @pl.when(j == n - 1)
    def _():
        for p in range(fin_ref.shape[2]):
            fin_ref[:, :, p] = pair_window(st_s, p)[...]


def _rwkv_scan(p_rw, s0, w0, wup, a0, aup, gup, k_k, k_a, r_k, red, bc2):
    bsz, t, cols = p_rw.shape
    c = RWKV_CHUNK
    nb = RWKV_BATCH_BLOCK
    n = t // c
    const2 = lambda b, i: (0, 0)
    const3 = lambda b, i: (0, 0, 0)
    state_blk = (nb, 2, RWKV_STATE_PAIRS, LANES, LANES)
    state_map = lambda b, i: (b, 0, 0, 0, 0)
    state_shape = jax.ShapeDtypeStruct((bsz, 2, RWKV_STATE_PAIRS, LANES, LANES), F32)
    tok_shape = jax.ShapeDtypeStruct((bsz, t, RWKV_WIDTH), F32)
    tok_f = pl.BlockSpec((nb, c, RWKV_WIDTH), lambda b, i: (b, i, 0))
    tok_b = pl.BlockSpec((nb, c, RWKV_WIDTH), lambda b, i: (b, n - 1 - i, 0))
    return pl.pallas_call(
        _rwkv_scan_kernel,
        grid=(bsz // nb, n),
        in_specs=[
            pl.BlockSpec((nb, c, cols), lambda b, i: (b, i, 0)),
            pl.BlockSpec((nb, c, cols), lambda b, i: (b, n - 1 - i, 0)),
            pl.BlockSpec(w0.shape, const2),
            pl.BlockSpec(wup.shape, const3), pl.BlockSpec(a0.shape, const2),
            pl.BlockSpec(aup.shape, const3), pl.BlockSpec(gup.shape, const2),
            pl.BlockSpec(k_k.shape, const2), pl.BlockSpec(k_a.shape, const2),
            pl.BlockSpec(r_k.shape, const2), pl.BlockSpec(red.shape, const2),
            pl.BlockSpec(bc2.shape, const2),
            pl.BlockSpec(state_blk, state_map, pipeline_mode=pl.Buffered(1))],
        out_specs=[tok_f, tok_b, tok_f, tok_f, pl.BlockSpec(state_blk, state_map)],
        out_shape=[tok_shape] * 4 + [state_shape],
        scratch_shapes=[pltpu.VMEM((nb, 2, RWKV_GROUPS, RWKV_GROUP_LANES, RWKV_GROUP_LANES), F32)],
        compiler_params=_params("parallel", "arbitrary"),
        name="rwkv_scan",
    )(p_rw, p_rw, w0, wup, a0, aup, gup, k_k, k_a, r_k, red, bc2, s0)


def _mlp_kernel(x_ref, ret_ref, yf_ref, yb_ref, bonus_ref, gate_ref, lnw_ref, lnb_ref, red_ref, bc_ref,
                g1_ref, sh2_ref, sc2_ref, g2_ref, n2g_ref, fg_ref,
                woa_ref, wob_ref, w1_ref, b1_ref, w2_ref, b2_ref, o_ref, *, ff_tile):
    mix_ret = jnp.dot(ret_ref[0], woa_ref[...], preferred_element_type=F32)
    red = red_ref[...]
    bc2 = bc_ref[...]
    inv_n = 1.0 / RWKV_HEAD_DIM
    blk = MERGE_ROW_BLOCK
    rows = [slice(r, r + blk) for r in range(0, x_ref.shape[1], blk)]

    def head_means(xs, two_piece):
        if two_piece:
            ss = [jnp.dot(jnp.concatenate(_split2(x), axis=0), red, preferred_element_type=F32) for x in xs]
            ss = [s[:blk] + s[blk:] for s in ss]
        else:
            ss = [jnp.dot(x.astype(BF16), red, preferred_element_type=F32) for x in xs]
        return [jnp.dot(jnp.concatenate(_split2(s * inv_n), axis=1), bc2, preferred_element_type=F32)
                for s in ss]

    ys = [yf_ref[0, r, :] + yb_ref[0, r, :] for r in rows]
    ycs = [y - m for y, m in zip(ys, head_means(ys, two_piece=True))]
    variances = head_means([yc * yc for yc in ycs], two_piece=False)
    rw = jnp.concatenate(
        [((yc * lax.rsqrt(var + GN_EPS) * lnw_ref[...] + lnb_ref[...] + bonus_ref[0, r, :])
          * gate_ref[0, r, :]).astype(BF16) for yc, var, r in zip(ycs, variances, rows)], axis=0)
    mix = mix_ret + jnp.dot(rw, wob_ref[...], preferred_element_type=F32)
    h1 = x_ref[0] + g1_ref[0] * mix
    n2 = (_rms(h1, n2g_ref[...] * (1.0 + sc2_ref[0])) + sh2_ref[0]).astype(BF16)
    d_ff = w1_ref.shape[1]
    slices = [slice(s * ff_tile, (s + 1) * ff_tile) for s in range(d_ff // ff_tile)]

    def up(sl):
        return jnp.dot(n2, w1_ref[:, sl], preferred_element_type=F32) + b1_ref[:, sl]

    acc = jnp.zeros(h1.shape, F32)
    hid = up(slices[0])
    for s, sl in enumerate(slices):
        nxt = up(slices[s + 1]) if s + 1 < len(slices) else None
        act = jnp.square(jnp.maximum(hid, 0.0)).astype(BF16)
        acc = acc + jnp.dot(act, w2_ref[sl, :], preferred_element_type=F32)
        hid = nxt
    h2 = h1 + g2_ref[0] * (acc + b2_ref[...])
    o_ref[0] = _rms(h2, fg_ref[...])


def _mlp(x, ret, yf, yb, bonus, gate, ln_w, ln_b, red, bc2, g1, sh2, sc2, g2, n2g, fg,
         woa, wob, w1, b1, w2, b2, tm):
    bsz, t, d = x.shape
    const = lambda b, i: (0, 0)
    single = pl.Buffered(1)
    mod = pl.BlockSpec((1, 1, d), lambda b, i: (b, 0, 0))
    cw = lambda a: pl.BlockSpec(a.shape, const, pipeline_mode=single)
    tok = lambda w: pl.BlockSpec((1, tm, w), lambda b, i: (b, i, 0))
    return pl.pallas_call(
        functools.partial(_mlp_kernel, ff_tile=1024),
        grid=(bsz, t // tm),
        in_specs=[tok(d), tok(RET_WIDTH), tok(RWKV_WIDTH), tok(RWKV_WIDTH), tok(RWKV_WIDTH),
                  tok(RWKV_WIDTH), cw(ln_w), cw(ln_b), cw(red), cw(bc2),
                  mod, mod, mod, mod, cw(n2g), cw(fg),
                  cw(woa), cw(wob), cw(w1), cw(b1), cw(w2), cw(b2)],
        out_specs=tok(d),
        out_shape=jax.ShapeDtypeStruct((bsz, t, d), F32),
        compiler_params=_params("parallel", "parallel"),
        name="outproj_mlp",
    )(x, ret, yf, yb, bonus, gate, ln_w, ln_b, red, bc2, g1, sh2, sc2, g2, n2g, fg,
      woa, wob, w1, b1, w2, b2)


def _rope_tables(t):
    half = RET_HEAD_DIM // 2
    inv = jnp.power(ROPE_BASE, -jnp.arange(0, half, 2, dtype=F32) / half)
    pos = jnp.arange(t)
    ang_r = (pos // GRID_W).astype(F32)[:, None] * inv[None, :]
    ang_c = (pos % GRID_W).astype(F32)[:, None] * inv[None, :]
    cos_t = jnp.concatenate([jnp.cos(ang_r), jnp.cos(ang_c)] * 2, axis=-1)
    sin_t = jnp.concatenate([-jnp.sin(ang_r), -jnp.sin(ang_c), jnp.sin(ang_r), jnp.sin(ang_c)], axis=-1)
    return cos_t, sin_t


def _rope_column_order(w_ret):
    d = w_ret.shape[0]
    quarter = RET_HEAD_DIM // 4
    qk = w_ret[:, :2 * RET_WIDTH].reshape(d, 2 * RET_HEADS, 2, 2, quarter)
    qk = jnp.swapaxes(qk, 2, 3).reshape(d, 2 * RET_WIDTH)
    return jnp.concatenate([qk, w_ret[:, 2 * RET_WIDTH:]], axis=1)


def kernel(x, c, ctx, c_ctx, w_ada, b_ada, norm1_g, norm2_g, w_in, ret_log_decay, rwkv_shift_mu,
           rwkv_w0, rwkv_w_up, rwkv_a0, rwkv_a_up, rwkv_g_up, rwkv_k_k, rwkv_k_a, rwkv_r_k,
           rwkv_ln_w, rwkv_ln_b, w_out, w_ff1, b_ff1, w_ff2, b_ff2, final_g):
    bsz, t, d = x.shape
    assert w_ada.shape[0] == 1, "single trunk layer"
    assert bsz % RWKV_BATCH_BLOCK == 0
    l = 0

    rows = -(-(bsz + 1) // SUBLANES) * SUBLANES
    cc = jnp.zeros((rows, d), F32).at[:bsz].set(c).at[bsz].set(c_ctx)
    mods = _ada(cc, w_ada[l], b_ada[l][None, :])
    sh1, sc1, g1, sh2, sc2, g2 = [m[:bsz, None, :] for m in jnp.split(mods, 6, axis=-1)]
    csh1, csc1 = [jnp.broadcast_to(m[bsz][None, None, :], (bsz, 1, d))
                  for m in jnp.split(mods, 6, axis=-1)[:2]]

    w_ret = _rope_column_order(w_in[l][:, :RET_COLS]).astype(BF16)
    w_rw = w_in[l][:, RET_COLS:].astype(BF16)
    n1g = norm1_g[l][None, :]
    mu = rwkv_shift_mu[l]
    p_ret, p_rw = _inproj(x, sh1, sc1, n1g, w_ret, w_rw, mu, tm=TOKEN_TILE)
    pc_ret, pc_rw = _inproj(ctx, csh1, csc1, n1g, w_ret, w_rw, mu, tm=ctx.shape[1])

    cos_t, sin_t = _rope_tables(t)
    ld = jnp.broadcast_to(ret_log_decay[l][:, :, None, None], (2, RET_HEADS, SUBLANES, LANES))
    ret_out = _retention(p_ret, pc_ret, cos_t, sin_t, ld)

    head_id = jnp.arange(RWKV_WIDTH) // RWKV_HEAD_DIM
    red = (head_id[:, None] == jnp.arange(LANES)[None, :]).astype(BF16)
    bc2 = jnp.concatenate([red.T, red.T], axis=0)
    rw_params = (rwkv_w0[l], rwkv_w_up[l].astype(BF16), rwkv_a0[l],
                 rwkv_a_up[l].astype(BF16), rwkv_g_up[l].astype(BF16),
                 rwkv_k_k[l][None, :], rwkv_k_a[l][None, :], rwkv_r_k[l][None, :], red, bc2)
    zeros = jnp.zeros((bsz, 2, RWKV_STATE_PAIRS, LANES, LANES), F32)
    s_ctx = _rwkv_scan(pc_rw, zeros, *rw_params)[4]
    yf, yb, bonus, gate, _ = _rwkv_scan(p_rw, s_ctx, *rw_params)

    wo = w_out[l].astype(BF16)
    return _mlp(x, ret_out, yf, yb, bonus, gate, rwkv_ln_w[l][None, :], rwkv_ln_b[l][None, :], red, bc2,
                g1, sh2, sc2, g2, norm2_g[l][None, :], final_g[None, :],
                wo[:RET_WIDTH], wo[RET_WIDTH:], w_ff1[l].astype(BF16), b_ff1[l][None, :],
                w_ff2[l].astype(BF16), b_ff2[l][None, :], tm=TOKEN_TILE)
```
